```python
import math, functools
import jax, jax.numpy as jnp
from jax import lax
import numpy as np


D_MODEL = 2048
BATCH = 1
SEQ = 16384
DEPTH = 4

GRID_W = 64
CTX_LEN = 256
EPS = 1e-6
ROPE_THETA = 10000.0
Q_BLOCK = 128
ROT_DIM = 64
MLA_HEADS = 8
MLA_NOPE = 128
MLA_ROPE = 64
MLA_V = 128
MLA_Q_RANK = 512
MLA_KV_RANK = 256
DIFF_HEADS = 4
DIFF_QK = 64
DIFF_V = 128
NA_HEADS = 4
NA_DIM = 128
NA_KH = 8
NA_KW = 16
NA_QCB = 16
NA_KCS = 32
N_EXPERTS = 32
TOP_K = 4
D_EXPERT = 768
SWIGLU_ALPHA = 1.702
SWIGLU_LIMIT = 7.0
EXPERT_BLOCK = 128

MIX_WIDTH = MLA_HEADS * MLA_V + DIFF_HEADS * DIFF_V + NA_HEADS * NA_DIM
IN_SPLITS = (MLA_Q_RANK, MLA_KV_RANK, MLA_ROPE,
             DIFF_HEADS * 2 * DIFF_QK, DIFF_HEADS * 2 * DIFF_QK, DIFF_HEADS * DIFF_V,
             NA_HEADS * NA_DIM, NA_HEADS * NA_DIM, NA_HEADS * NA_DIM)
IN_WIDTH = sum(IN_SPLITS)
F32 = jnp.float32

kernel_name = 'hybrid_mla_diff_natten_moe_dit'


def rmsnorm(x, g):
    xf = x.astype(F32)
    y = xf * lax.rsqrt(jnp.mean(xf * xf, axis=-1, keepdims=True) + EPS)
    return (y * g.astype(F32)).astype(x.dtype)


def modulate(h, shift, scale):
    return h * (1 + scale) + shift


def axial_rope_tables(n_lat, n_ctx):
    quarter = ROT_DIM // 4
    inv = 1.0 / (ROPE_THETA ** (jnp.arange(quarter, dtype=F32) / quarter))
    t = jnp.arange(n_lat)
    row = (t // GRID_W).astype(F32)
    col = (t % GRID_W).astype(F32)
    ang = jnp.concatenate([row[:, None] * inv, col[:, None] * inv], axis=-1)
    ang = jnp.concatenate([ang, jnp.zeros((n_ctx, ROT_DIM // 2), F32)], axis=0)
    return jnp.cos(ang), jnp.sin(ang)


def apply_rope(x, cos, sin):
    half = x.shape[-1] // 2
    expand = (1,) * (x.ndim - 3)
    cos = cos.reshape((cos.shape[0],) + expand + (half,))
    sin = sin.reshape((sin.shape[0],) + expand + (half,))
    xf = x.astype(F32)
    x1, x2 = xf[..., :half], xf[..., half:]
    return jnp.concatenate([x1 * cos - x2 * sin, x1 * sin + x2 * cos], axis=-1).astype(x.dtype)


def map_query_blocks(fn, q):
    B, N = q.shape[:2]
    nb = N // Q_BLOCK
    qb = jnp.moveaxis(q.reshape((B, nb, Q_BLOCK) + q.shape[2:]), 1, 0)
    o = lax.map(fn, qb)
    return jnp.moveaxis(o, 0, 1).reshape((B, N) + o.shape[3:])


def softmax_attention(q, k, v, scale):
    s = jnp.einsum('bqhd,bkhd->bhqk', q, k).astype(F32) * scale
    p = jax.nn.softmax(s, axis=-1).astype(v.dtype)
    return jnp.einsum('bhqk,bkhd->bqhd', p, v)


def differential_attention(q, k, v, lam, scale):
    s = jnp.einsum('bqhtd,bkhtd->bthqk', q, k).astype(F32) * scale
    p = jax.nn.softmax(s, axis=-1)
    w = (p[:, 0] - lam * p[:, 1]).astype(v.dtype)
    return jnp.einsum('bhqk,bkhd->bqhd', w, v)


def column_window_tables():
    ncb = GRID_W // NA_QCB
    j = np.arange(ncb)
    cs = np.clip(j * NA_QCB - NA_KW // 2, 0, GRID_W - NA_KCS)
    col_idx = cs[:, None] + np.arange(NA_KCS)
    qc = j[:, None] * NA_QCB + np.arange(NA_QCB)
    wc = np.clip(qc - NA_KW // 2, 0, GRID_W - NA_KW)[..., None]
    kc = col_idx[:, None, :]
    mask = (kc >= wc) & (kc < wc + NA_KW)
    rel = np.clip(kc - qc[..., None] + NA_KW - 1, 0, 2 * NA_KW - 2)
    return col_idx, mask, rel


def neighborhood_attention(q, k, v, k_ctx, v_ctx, rpb, scale):
    B, N, H, dh = q.shape
    rows = N // GRID_W
    kh = min(NA_KH, rows)
    col_idx, col_mask, col_rel = column_window_tables()
    ncb = GRID_W // NA_QCB
    qg = q.reshape(B, rows, GRID_W, H, dh)
    kg = k.reshape(B, rows, GRID_W, H, dh)
    vg = v.reshape(B, rows, GRID_W, H, dh)
    rpb_f = rpb.astype(F32)
    mask = jnp.asarray(col_mask)[None, None, :, :, None, :]
    n_loc = kh * NA_KCS

    def one_row(r):
        rs = jnp.clip(r - kh // 2, 0, rows - kh)
        q_r = lax.dynamic_index_in_dim(qg, r, axis=1, keepdims=False).reshape(B, ncb, NA_QCB, H, dh)
        k_b = lax.dynamic_slice_in_dim(kg, rs, kh, axis=1)[:, :, col_idx]
        v_b = lax.dynamic_slice_in_dim(vg, rs, kh, axis=1)[:, :, col_idx]
        s_loc = jnp.einsum('bjqhd,brjchd->bhjqrc', q_r, k_b).astype(F32) * scale
        row_rel = rs + jnp.arange(kh) - r + (NA_KH - 1)
        bias = rpb_f[:, row_rel][:, :, col_rel].transpose(0, 2, 3, 1, 4)
        s_loc = jnp.where(mask, s_loc + bias[None], -jnp.inf).reshape(B, H, ncb, NA_QCB, n_loc)
        s_ctx = jnp.einsum('bjqhd,bkhd->bhjqk', q_r, k_ctx).astype(F32) * scale
        p = jax.nn.softmax(jnp.concatenate([s_loc, s_ctx], axis=-1), axis=-1).astype(v.dtype)
        p_loc = p[..., :n_loc].reshape(B, H, ncb, NA_QCB, kh, NA_KCS)
        o = (jnp.einsum('bhjqrc,brjchd->bjqhd', p_loc, v_b)
             + jnp.einsum('bhjqk,bkhd->bjqhd', p[..., n_loc:], v_ctx))
        return o.reshape(B, GRID_W, H, dh)

    o = lax.map(one_row, jnp.arange(rows))
    return jnp.moveaxis(o, 0, 1).reshape(B, N, H, dh)


def hybrid_mixer(h, hc, cos, sin, w_in, mla_qa_norm, w_uq, mla_kva_norm, w_ukv, mla_q_gain,
                 mla_knope_gain, mla_kpe_gain, diff_q_gain, diff_k_gain, diff_lambda, diff_subln,
                 na_q_gain, na_k_gain, na_rpb, w_out, lam_init, with_ctx_out):
    B, N, _ = h.shape
    T = N + hc.shape[1]
    proj = jnp.concatenate([h, hc], axis=1) @ w_in
    split_at = np.cumsum(IN_SPLITS)[:-1].tolist()
    qa, kva, kpe, dq, dk, dv, nq, nk, nv = jnp.split(proj, split_at, axis=-1)

    q_a = (rmsnorm(qa, mla_qa_norm) @ w_uq).reshape(B, T, MLA_HEADS, MLA_NOPE + MLA_ROPE)
    q_a = rmsnorm(q_a, mla_q_gain)
    q_a = jnp.concatenate([q_a[..., :MLA_NOPE], apply_rope(q_a[..., MLA_NOPE:], cos, sin)], axis=-1)
    kv = (rmsnorm(kva, mla_kva_norm) @ w_ukv).reshape(B, T, MLA_HEADS, MLA_NOPE + MLA_V)
    k_nope = rmsnorm(kv[..., :MLA_NOPE], mla_knope_gain)
    v_a = kv[..., MLA_NOPE:]
    k_pe = apply_rope(rmsnorm(kpe, mla_kpe_gain), cos, sin)
    k_a = jnp.concatenate([k_nope, jnp.broadcast_to(k_pe[:, :, None], (B, T, MLA_HEADS, MLA_ROPE))], axis=-1)
    mla_scale = (MLA_NOPE + MLA_ROPE) ** -0.5
    o_a = map_query_blocks(lambda qi: softmax_attention(qi, k_a, v_a, mla_scale), q_a[:, :N])

    q_b = apply_rope(rmsnorm(dq.reshape(B, T, DIFF_HEADS, 2, DIFF_QK), diff_q_gain), cos, sin)
    k_b = apply_rope(rmsnorm(dk.reshape(B, T, DIFF_HEADS, 2, DIFF_QK), diff_k_gain), cos, sin)
    v_b = dv.reshape(B, T, DIFF_HEADS, DIFF_V)
    lf = diff_lambda.astype(F32)
    lam = jnp.exp(jnp.sum(lf[0] * lf[1])) - jnp.exp(jnp.sum(lf[2] * lf[3])) + lam_init
    diff_scale = DIFF_QK ** -0.5
    o_b = map_query_blocks(lambda qi: differential_attention(qi, k_b, v_b, lam, diff_scale), q_b[:, :N])
    o_b = rmsnorm(o_b, diff_subln) * (1 - lam_init)

    q_c = rmsnorm(nq.reshape(B, T, NA_HEADS, NA_DIM), na_q_gain)
    k_c = rmsnorm(nk.reshape(B, T, NA_HEADS, NA_DIM), na_k_gain)
    v_c = nv.reshape(B, T, NA_HEADS, NA_DIM)
    na_scale = NA_DIM ** -0.5
    o_c = neighborhood_attention(q_c[:, :N], k_c[:, :N], v_c[:, :N], k_c[:, N:], v_c[:, N:], na_rpb, na_scale)

    o = jnp.concatenate([o_a.reshape(B, N, -1), o_b.reshape(B, N, -1), o_c.reshape(B, N, -1)], axis=-1) @ w_out
    if not with_ctx_out:
        return o, None
    oc_a = softmax_attention(q_a[:, N:], k_a[:, N:], v_a[:, N:], mla_scale)
    oc_b = rmsnorm(differential_attention(q_b[:, N:], k_b[:, N:], v_b[:, N:], lam, diff_scale), diff_subln) * (1 - lam_init)
    oc_c = softmax_attention(q_c[:, N:], k_c[:, N:], v_c[:, N:], na_scale)
    C = T - N
    oc = jnp.concatenate([oc_a.reshape(B, C, -1), oc_b.reshape(B, C, -1), oc_c.reshape(B, C, -1)], axis=-1) @ w_out
    return o, oc


def moe_ffn(h, router_w, router_b, w_gu, b_gu, w_down, b_down):
    shp = h.shape
    xf = h.reshape(-1, shp[-1])
    n = xf.shape[0]
    logits = (xf @ router_w).astype(F32) + router_b.astype(F32)
    top_v, top_e = lax.top_k(logits, TOP_K)
    gates = jax.nn.softmax(top_v, axis=-1)
    n_assign = n * TOP_K
    flat_e = top_e.reshape(-1)
    flat_tok = jnp.arange(n_assign, dtype=jnp.int32) // TOP_K
    order = jnp.argsort(flat_e)
    e_sorted = flat_e[order]
    counts = jnp.bincount(flat_e, length=N_EXPERTS)
    padded = (counts + EXPERT_BLOCK - 1) // EXPERT_BLOCK * EXPERT_BLOCK
    start = jnp.cumsum(counts) - counts
    pad_end = jnp.cumsum(padded)
    pad_start = pad_end - padded
    dest = pad_start[e_sorted] + jnp.arange(n_assign, dtype=jnp.int32) - start[e_sorted]
    n_blocks = -(-n_assign // EXPERT_BLOCK) + N_EXPERTS
    cap = n_blocks * EXPERT_BLOCK
    row_tok = jnp.full((cap,), n, jnp.int32).at[dest].set(flat_tok[order])
    row_gate = jnp.zeros((cap,), F32).at[dest].set(gates.reshape(-1)[order])
    blk_e = jnp.minimum(jnp.searchsorted(pad_end, jnp.arange(n_blocks) * EXPERT_BLOCK, side='right'), N_EXPERTS - 1)
    xpad = jnp.concatenate([xf, jnp.zeros((1, xf.shape[1]), xf.dtype)], axis=0)
    xb = xpad[row_tok].reshape(n_blocks, EXPERT_BLOCK, xf.shape[1])

    def expert_block(args):
        xi, e = args
        gu = xi @ w_gu[e] + b_gu[e]
        glu = jnp.minimum(gu[..., ::2], SWIGLU_LIMIT)
        lin = jnp.clip(gu[..., 1::2], -SWIGLU_LIMIT, SWIGLU_LIMIT)
        act = glu * jax.nn.sigmoid(SWIGLU_ALPHA * glu) * (lin + 1)
        return act @ w_down[e] + b_down[e]

    yb = lax.map(expert_block, (xb, blk_e)).reshape(cap, -1)
    y = jax.ops.segment_sum(yb * row_gate[:, None].astype(yb.dtype), row_tok, num_segments=n + 1)[:n]
    return y.reshape(shp)


def setup_inputs(seed: int = 0) -> dict:
    key = jax.random.key(seed)
    ks = jax.random.split(key, 32)
    L, D = DEPTH, D_MODEL

    def nrm(k, shape, scale):
        return jax.random.normal(k, shape, F32) * scale

    def gain(k, shape):
        return 1.0 + 0.05 * jax.random.normal(k, shape, F32)

    return {
        'x': nrm(ks[0], (BATCH, SEQ, D), 1.0),
        'c': nrm(ks[1], (BATCH, D), 1.0),
        'ctx': nrm(ks[2], (BATCH, CTX_LEN, D), 1.0),
        'c_ctx': nrm(ks[3], (D,), 1.0),
        'w_ada': nrm(ks[4], (L, D, 6 * D), 0.5 * D ** -0.5),
        'b_ada': nrm(ks[5], (L, 6 * D), 0.01),
        'attn_norm': gain(ks[6], (L, D)),
        'ffn_norm': gain(ks[7], (L, D)),
        'w_in': nrm(ks[8], (L, D, IN_WIDTH), D ** -0.5),
        'mla_qa_norm': gain(ks[9], (L, MLA_Q_RANK)),
        'w_uq': nrm(ks[10], (L, MLA_Q_RANK, MLA_HEADS * (MLA_NOPE + MLA_ROPE)), MLA_Q_RANK ** -0.5),
        'mla_kva_norm': gain(ks[11], (L, MLA_KV_RANK)),
        'w_ukv': nrm(ks[12], (L, MLA_KV_RANK, MLA_HEADS * (MLA_NOPE + MLA_V)), MLA_KV_RANK ** -0.5),
        'mla_q_gain': gain(ks[13], (L, MLA_NOPE + MLA_ROPE)),
        'mla_knope_gain': gain(ks[14], (L, MLA_NOPE)),
        'mla_kpe_gain': gain(ks[15], (L, MLA_ROPE)),
        'diff_q_gain': gain(ks[16], (L, DIFF_QK)),
        'diff_k_gain': gain(ks[17], (L, DIFF_QK)),
        'diff_lambda': nrm(ks[18], (L, 4, DIFF_QK), 0.1),
        'diff_subln': gain(ks[19], (L, DIFF_V)),
        'na_q_gain': gain(ks[20], (L, NA_DIM)),
        'na_k_gain': gain(ks[21], (L, NA_DIM)),
        'na_rpb': nrm(ks[22], (L, NA_HEADS, 2 * NA_KH - 1, 2 * NA_KW - 1), 0.1),
        'w_out': nrm(ks[23], (L, MIX_WIDTH, D), MIX_WIDTH ** -0.5),
        'router_w': nrm(ks[24], (L, D, N_EXPERTS), D ** -0.5),
        'router_b': nrm(ks[25], (L, N_EXPERTS), 0.01),
        'w_gu': nrm(ks[26], (L, N_EXPERTS, D, 2 * D_EXPERT), D ** -0.5),
        'b_gu': nrm(ks[27], (L, N_EXPERTS, 2 * D_EXPERT), 0.01),
        'w_down': nrm(ks[28], (L, N_EXPERTS, D_EXPERT, D), D_EXPERT ** -0.5),
        'b_down': nrm(ks[29], (L, N_EXPERTS, D), 0.01),
    }


def reference(x, c, ctx, c_ctx, w_ada, b_ada, attn_norm, ffn_norm, w_in, mla_qa_norm, w_uq,
              mla_kva_norm, w_ukv, mla_q_gain, mla_knope_gain, mla_kpe_gain, diff_q_gain, diff_k_gain,
              diff_lambda, diff_subln, na_q_gain, na_k_gain, na_rpb, w_out, router_w, router_b,
              w_gu, b_gu, w_down, b_down):
    n_lat = x.shape[1]
    n_ctx = ctx.shape[1]
    cos, sin = axial_rope_tables(n_lat, n_ctx)
    silu_c = jax.nn.silu(c)
    silu_cc = jax.nn.silu(c_ctx)
    xc = ctx
    for l in range(DEPTH):
        with_ctx_out = l < DEPTH - 1
        lam_init = 0.8 - 0.6 * math.exp(-0.3 * l)
        mod = silu_c @ w_ada[l] + b_ada[l]
        mod_c = silu_cc @ w_ada[l] + b_ada[l]
        sh1, sc1, g1, sh2, sc2, g2 = [m[:, None, :] for m in jnp.split(mod, 6, axis=-1)]
        sh1c, sc1c, g1c, sh2c, sc2c, g2c = jnp.split(mod_c, 6, axis=-1)
        h = modulate(rmsnorm(x, attn_norm[l]), sh1, sc1)
        hc = modulate(rmsnorm(xc, attn_norm[l]), sh1c, sc1c)
        o, oc = hybrid_mixer(h, hc, cos, sin, w_in[l], mla_qa_norm[l], w_uq[l], mla_kva_norm[l], w_ukv[l],
                             mla_q_gain[l], mla_knope_gain[l], mla_kpe_gain[l], diff_q_gain[l], diff_k_gain[l],
                             diff_lambda[l], diff_subln[l], na_q_gain[l], na_k_gain[l], na_rpb[l], w_out[l],
                             lam_init, with_ctx_out)
        x = x + g1 * o
        h = modulate(rmsnorm(x, ffn_norm[l]), sh2, sc2)
        if with_ctx_out:
            xc = xc + g1c * oc
            hc = modulate(rmsnorm(xc, ffn_norm[l]), sh2c, sc2c)
            y = moe_ffn(jnp.concatenate([h, hc], axis=1), router_w[l], router_b[l], w_gu[l], b_gu[l], w_down[l], b_down[l])
            x = x + g2 * y[:, :n_lat]
            xc = xc + g2c * y[:, n_lat:]
        else:
            x = x + g2 * moe_ffn(h, router_w[l], router_b[l], w_gu[l], b_gu[l], w_down[l], b_down[l])
    return x
```

```python
import functools
import math

import numpy as np
import jax
import jax.numpy as jnp
from jax import lax
from jax.experimental import pallas as pl
from jax.experimental.pallas import tpu as pltpu

F32 = jnp.float32
BF16 = jnp.bfloat16

GRID_W = 64
EPS = 1e-6
ROPE_THETA = 10000.0
ROT_DIM = 64
MLA_HEADS = 8
MLA_NOPE = 128
MLA_ROPE = 64
MLA_V = 128
MLA_Q_RANK = 512
MLA_KV_RANK = 256
MLA_HEAD_PAD = 256
DIFF_HEADS = 4
DIFF_QK = 64
DIFF_V = 128
NA_HEADS = 4
NA_DIM = 128
NA_KH = 8
NA_KW = 16
NA_ROWS_PER_GROUP = 4
N_EXPERTS = 32
TOP_K = 4
D_EXPERT = 768
SWIGLU_ALPHA = 1.702
SWIGLU_LIMIT = 7.0
MOE_BLOCK = 256
NEG_BIG = -1e30

LANES = 128
VMEM_LIMIT = 56 * 1024 * 1024

NT_DIMS = (((1,), (1,)), ((), ()))


def _pick_tile(m, target, mult=8):
    best = None
    for t in range(mult, min(m, target) + 1, mult):
        if m % t == 0:
            best = t
    assert best is not None, (m, target)
    return best


def _mm_kernel(a_ref, b_ref, o_ref):
    a = a_ref[...].astype(BF16)
    b = b_ref[...].astype(BF16)
    o_ref[...] = jnp.dot(a, b, preferred_element_type=F32).astype(o_ref.dtype)


def mm(a, b, out_dtype=F32, tm_target=1280, tn_target=512, name="mm"):
    M, K = a.shape
    K2, N = b.shape
    assert K == K2
    tm = _pick_tile(M, tm_target)
    tn = _pick_tile(N, tn_target, LANES)
    return pl.pallas_call(
        _mm_kernel,
        grid=(M // tm, N // tn),
        in_specs=[pl.BlockSpec((tm, K), lambda i, j: (i, 0)),
                  pl.BlockSpec((K, tn), lambda i, j: (0, j))],
        out_specs=pl.BlockSpec((tm, tn), lambda i, j: (i, j)),
        out_shape=jax.ShapeDtypeStruct((M, N), out_dtype),
        compiler_params=pltpu.CompilerParams(
            dimension_semantics=("parallel", "arbitrary"), vmem_limit_bytes=VMEM_LIMIT),
        name=name,
    )(a, b)


def _mm_f32_kernel(a_ref, b_ref, o_ref):
    o_ref[...] = jnp.dot(a_ref[...], b_ref[...], preferred_element_type=F32,
                         precision=lax.Precision.HIGHEST)


def mm_f32(a, b, tm_target=512, name="mm_f32"):
    M, K = a.shape
    _, N = b.shape
    tm = _pick_tile(M, tm_target)
    return pl.pallas_call(
        _mm_f32_kernel,
        grid=(M // tm,),
        in_specs=[pl.BlockSpec((tm, K), lambda i: (i, 0)),
                  pl.BlockSpec((K, N), lambda i: (0, 0))],
        out_specs=pl.BlockSpec((tm, N), lambda i: (i, 0)),
        out_shape=jax.ShapeDtypeStruct((M, N), F32),
        compiler_params=pltpu.CompilerParams(
            dimension_semantics=("parallel",), vmem_limit_bytes=VMEM_LIMIT),
        name=name,
    )(a, b)


def _flash_kernel(*refs, tk, n_chunks, diff, lam_scale):
    if diff:
        q_ref, k_ref, v_ref, lam_ref, g_ref, o_ref = refs
    else:
        q_ref, k_ref, v_ref, o_ref = refs
    q = q_ref[...]
    tq = q.shape[0]
    if diff:
        lane = lax.broadcasted_iota(jnp.int32, q.shape, 1)
        zero = jnp.zeros_like(q)
        q = jnp.concatenate([jnp.where(lane < DIFF_QK, q, zero),
                             jnp.where(lane >= DIFF_QK, q, zero)], axis=0)
    rows = q.shape[0]
    dv = v_ref.shape[1]

    def body(c, carry):
        m, l, acc = carry
        start = pl.multiple_of(c * tk, tk)
        ks = k_ref[pl.ds(start, tk), :]
        vs = v_ref[pl.ds(start, tk), :]
        s = lax.dot_general(q, ks, NT_DIMS, preferred_element_type=F32)
        m_new = jnp.maximum(m, jnp.max(s, axis=-1, keepdims=True))
        alpha = jnp.exp(m - m_new)
        p = jnp.exp(s - m_new)
        l = alpha * l + jnp.sum(p, axis=-1, keepdims=True)
        acc = alpha * acc + jnp.dot(p.astype(BF16), vs, preferred_element_type=F32)
        return m_new, l, acc

    m0 = jnp.full((rows, 1), -jnp.inf, F32)
    l0 = jnp.zeros((rows, 1), F32)
    a0 = jnp.zeros((rows, dv), F32)
    if n_chunks == 1:
        m, l, acc = body(0, (m0, l0, a0))
    else:
        m, l, acc = lax.fori_loop(0, n_chunks, body, (m0, l0, a0))
    o = acc / l
    if diff:
        o = o[:tq] - lam_ref[...] * o[tq:]
        ms = jnp.mean(o * o, axis=-1, keepdims=True)
        o = o * lax.rsqrt(ms + EPS) * g_ref[...] * lam_scale
    o_ref[...] = o.astype(o_ref.dtype)


def flash(q, k, v, *, heads, dk, v_col, n_q, q_start, n_k, k_start, tq_target, tk_target,
          diff_args=None, name="flash"):
    tq = _pick_tile(math.gcd(n_q, q_start) if q_start else n_q, tq_target)
    tk = _pick_tile(n_k, tk_target, LANES)
    assert k_start % n_k == 0
    q_blk0 = q_start // tq
    k_blk0 = k_start // n_k
    in_specs = [pl.BlockSpec((tq, dk), lambda h, i: (q_blk0 + i, h)),
                pl.BlockSpec((n_k, dk), lambda h, i: (k_blk0, h)),
                pl.BlockSpec((n_k, LANES), lambda h, i: (k_blk0, v_col(h)))]
    args = [q, k, v]
    diff = diff_args is not None
    lam_scale = 1.0
    if diff:
        lam_row, gain_row, lam_scale = diff_args
        in_specs += [pl.BlockSpec((1, LANES), lambda h, i: (0, 0)),
                     pl.BlockSpec((1, LANES), lambda h, i: (0, 0))]
        args += [lam_row, gain_row]
    kern = functools.partial(_flash_kernel, tk=tk, n_chunks=n_k // tk, diff=diff,
                             lam_scale=lam_scale)
    return pl.pallas_call(
        kern,
        grid=(heads, n_q // tq),
        in_specs=in_specs,
        out_specs=pl.BlockSpec((tq, LANES), lambda h, i: (i, h)),
        out_shape=jax.ShapeDtypeStruct((n_q, heads * LANES), BF16),
        compiler_params=pltpu.CompilerParams(
            dimension_semantics=("parallel", "arbitrary"), vmem_limit_bytes=VMEM_LIMIT),
        name=name,
    )(*args)


def _na_kernel(q_ref, k0_ref, k1_ref, k2_ref, v0_ref, v1_ref, v2_ref, kc_ref, vc_ref, bias_ref,
               o_ref):
    k_refs = (k0_ref, k1_ref, k2_ref, kc_ref)
    v_refs = (v0_ref, v1_ref, v2_ref, vc_ref)
    blk = k0_ref.shape[0]
    for h in range(NA_HEADS):
        cols = slice(h * NA_DIM, (h + 1) * NA_DIM)
        q = q_ref[:, cols]
        s_parts = []
        for b in range(4):
            s = lax.dot_general(q, k_refs[b][:, cols], NT_DIMS, preferred_element_type=F32)
            if b < 3:
                s = s + bias_ref[0, h, :, b * blk:(b + 1) * blk]
            s_parts.append(s)
        m = s_parts[3].max(axis=-1, keepdims=True)
        for b in range(3):
            m = jnp.maximum(m, s_parts[b].max(axis=-1, keepdims=True))
        l = jnp.zeros_like(m)
        acc = jnp.zeros((q.shape[0], NA_DIM), F32)
        for b in range(4):
            p = jnp.exp(s_parts[b] - m)
            l = l + p.sum(axis=-1, keepdims=True)
            acc = acc + jnp.dot(p.astype(BF16), v_refs[b][:, cols], preferred_element_type=F32)
        o_ref[:, cols] = (acc / l).astype(o_ref.dtype)


def _na_bias_tables(rpb, rows):
    R = NA_ROWS_PER_GROUP
    G = rows // R
    assert rows % R == 0 and G >= 3 and rows >= NA_KH and NA_KH == 2 * R
    tabs = []
    for g in (0, 1, G - 1):
        j = np.arange(R)[:, None, None, None, None]
        qc = np.arange(GRID_W)[None, :, None, None, None]
        b = np.arange(3)[None, None, :, None, None]
        kr = np.arange(R)[None, None, None, :, None]
        kc = np.arange(GRID_W)[None, None, None, None, :]
        r = R * g + j
        rs = np.clip(r - NA_KH // 2, 0, rows - NA_KH)
        keyrow = R * (g - 1 + b) + kr
        wc = np.clip(qc - NA_KW // 2, 0, GRID_W - NA_KW)
        valid = (keyrow >= rs) & (keyrow < rs + NA_KH) & (kc >= wc) & (kc < wc + NA_KW)
        row_rel = np.clip(keyrow - r + NA_KH - 1, 0, 2 * NA_KH - 2)
        col_rel = np.clip(kc - qc + NA_KW - 1, 0, 2 * NA_KW - 2)
        shape = (R, GRID_W, 3, R, GRID_W)
        valid = np.broadcast_to(valid, shape).reshape(R * GRID_W, 3 * R * GRID_W)
        row_rel = np.broadcast_to(row_rel, shape).reshape(R * GRID_W, 3 * R * GRID_W)
        col_rel = np.broadcast_to(col_rel, shape).reshape(R * GRID_W, 3 * R * GRID_W)
        bias = rpb.astype(F32)[:, row_rel, col_rel]
        tabs.append(jnp.where(valid[None], bias, NEG_BIG))
    return jnp.stack(tabs)


def neighborhood(q, k, v, bias_tabs, n_lat):
    blk = NA_ROWS_PER_GROUP * GRID_W
    G = n_lat // blk
    n_ctx = q.shape[0] - n_lat
    assert n_ctx == blk, "context length must equal one key block"
    W = NA_HEADS * NA_DIM

    def kv_spec(off):
        return pl.BlockSpec((blk, W), lambda g: (jnp.clip(g + off, 0, G - 1), 0))

    ctx_spec = pl.BlockSpec((blk, W), lambda g: (G, 0))
    bias_spec = pl.BlockSpec(
        (1, NA_HEADS, blk, 3 * blk),
        lambda g: (jnp.where(g == 0, 0, jnp.where(g == G - 1, 2, 1)), 0, 0, 0))
    return pl.pallas_call(
        _na_kernel,
        grid=(G,),
        in_specs=[pl.BlockSpec((blk, W), lambda g: (g, 0)),
                  kv_spec(-1), kv_spec(0), kv_spec(1),
                  kv_spec(-1), kv_spec(0), kv_spec(1),
                  ctx_spec, ctx_spec, bias_spec],
        out_specs=pl.BlockSpec((blk, W), lambda g: (g, 0)),
        out_shape=jax.ShapeDtypeStruct((n_lat, W), BF16),
        compiler_params=pltpu.CompilerParams(
            dimension_semantics=("parallel",), vmem_limit_bytes=VMEM_LIMIT),
        name="na",
    )(q, k, k, k, v, v, v, k, v, bias_tabs)


def _moe_kernel(blk_e_ref, n_used_ref, x_ref, wg_ref, wl_ref, bg_ref, bl_ref, wd_ref, bd_ref,
                o_ref):
    i = pl.program_id(0)

    @pl.when(i < n_used_ref[0])
    def _():
        x = x_ref[...]
        g = jnp.dot(x, wg_ref[0], preferred_element_type=F32) + bg_ref[0]
        u = jnp.dot(x, wl_ref[0], preferred_element_type=F32) + bl_ref[0]
        glu = jnp.minimum(g, SWIGLU_LIMIT)
        lin = jnp.clip(u, -SWIGLU_LIMIT, SWIGLU_LIMIT)
        act = glu * jax.nn.sigmoid(SWIGLU_ALPHA * glu) * (lin + 1.0)
        y = jnp.dot(act.astype(BF16), wd_ref[0], preferred_element_type=F32) + bd_ref[0]
        o_ref[...] = y.astype(o_ref.dtype)

    @pl.when(i >= n_used_ref[0])
    def _():
        o_ref[...] = jnp.zeros_like(o_ref)


def moe_experts(xb, blk_e, n_used, wg, wl, bg, bl, wd, bd):
    cap, D = xb.shape
    n_blocks = cap // MOE_BLOCK
    F = wg.shape[-1]
    grid_spec = pltpu.PrefetchScalarGridSpec(
        num_scalar_prefetch=2,
        grid=(n_blocks,),
        in_specs=[pl.BlockSpec((MOE_BLOCK, D), lambda i, e, n: (i, 0)),
                  pl.BlockSpec((1, D, F), lambda i, e, n: (e[i], 0, 0)),
                  pl.BlockSpec((1, D, F), lambda i, e, n: (e[i], 0, 0)),
                  pl.BlockSpec((1, 1, F), lambda i, e, n: (e[i], 0, 0)),
                  pl.BlockSpec((1, 1, F), lambda i, e, n: (e[i], 0, 0)),
                  pl.BlockSpec((1, F, D), lambda i, e, n: (e[i], 0, 0)),
                  pl.BlockSpec((1, 1, D), lambda i, e, n: (e[i], 0, 0))],
        out_specs=pl.BlockSpec((MOE_BLOCK, D), lambda i, e, n: (i, 0)),
    )
    return pl.pallas_call(
        _moe_kernel,
        grid_spec=grid_spec,
        out_shape=jax.ShapeDtypeStruct((cap, D), F32),
        compiler_params=pltpu.CompilerParams(
            dimension_semantics=("arbitrary",), vmem_limit_bytes=VMEM_LIMIT),
        name="moe_experts",
    )(blk_e, n_used, xb, wg, wl, bg, bl, wd, bd)


def moe_ffn(h, router_w_pad, router_b, wg, wl, bg, bl, wd, bd):
    n, D = h.shape
    logits = mm_f32(h, router_w_pad, name="router")[:, :N_EXPERTS] + router_b.astype(F32)
    top_v, top_e = lax.top_k(logits, TOP_K)
    gates = jax.nn.softmax(top_v, axis=-1)
    n_assign = n * TOP_K
    flat_e = top_e.reshape(-1).astype(jnp.int32)
    order = jnp.argsort(flat_e).astype(jnp.int32)
    e_sorted = flat_e[order]
    counts = jnp.bincount(flat_e, length=N_EXPERTS).astype(jnp.int32)
    padded = (counts + MOE_BLOCK - 1) // MOE_BLOCK * MOE_BLOCK
    start = jnp.cumsum(counts) - counts
    pad_end = jnp.cumsum(padded)
    pad_start = pad_end - padded
    dest = pad_start[e_sorted] + jnp.arange(n_assign, dtype=jnp.int32) - start[e_sorted]
    n_blocks = -(-n_assign // MOE_BLOCK) + N_EXPERTS
    cap = n_blocks * MOE_BLOCK
    row_tok = jnp.full((cap,), n, jnp.int32).at[dest].set(order // TOP_K)
    blk_e = jnp.minimum(
        jnp.searchsorted(pad_end, jnp.arange(n_blocks, dtype=jnp.int32) * MOE_BLOCK, side='right'),
        N_EXPERTS - 1).astype(jnp.int32)
    n_used = (pad_end[-1:] // MOE_BLOCK).astype(jnp.int32)
    xpad = jnp.concatenate([h.astype(BF16), jnp.zeros((1, D), BF16)], axis=0)
    xb = xpad[row_tok]
    yb = moe_experts(xb, blk_e, n_used, wg, wl, bg, bl, wd, bd)
    pos = jnp.zeros((n_assign,), jnp.int32).at[order].set(dest).reshape(n, TOP_K)
    y = jnp.sum(yb[pos] * gates[:, :, None], axis=1)
    return y


def _rmsnorm(x, g):
    xf = x.astype(F32)
    return xf * lax.rsqrt(jnp.mean(xf * xf, axis=-1, keepdims=True) + EPS) * g.astype(F32)


def _rope_tables(n_lat, n_ctx):
    quarter = ROT_DIM // 4
    inv = 1.0 / (ROPE_THETA ** (jnp.arange(quarter, dtype=F32) / quarter))
    t = jnp.arange(n_lat)
    row = (t // GRID_W).astype(F32)
    col = (t % GRID_W).astype(F32)
    ang = jnp.concatenate([row[:, None] * inv, col[:, None] * inv], axis=-1)
    ang = jnp.concatenate([ang, jnp.zeros((n_ctx, ROT_DIM // 2), F32)], axis=0)
    return jnp.cos(ang), jnp.sin(ang)


def _rope(x, cos, sin):
    half = x.shape[-1] // 2
    expand = (1,) * (x.ndim - 2)
    cos = cos.reshape((cos.shape[0],) + expand + (half,))
    sin = sin.reshape((sin.shape[0],) + expand + (half,))
    x1, x2 = x[..., :half], x[..., half:]
    return jnp.concatenate([x1 * cos - x2 * sin, x1 * sin + x2 * cos], axis=-1)


_P_QA, _P_KVA, _P_KPE, _P_DQ, _P_DK, _P_DV, _P_NQ, _P_NK, _P_NV, _P_END = (
    0, 512, 768, 1024, 1536, 2048, 2560, 3072, 3584, 4096)


def _pad_w_in(w):
    D = w.shape[0]
    a = MLA_Q_RANK + MLA_KV_RANK + MLA_ROPE
    return jnp.concatenate([w[:, :a], jnp.zeros((D, _P_DQ - a), w.dtype), w[:, a:]], axis=1)


def _pad_w_uq(w):
    r = w.shape[0]
    w3 = w.reshape(r, MLA_HEADS, MLA_NOPE + MLA_ROPE)
    w3 = jnp.concatenate(
        [w3, jnp.zeros((r, MLA_HEADS, MLA_HEAD_PAD - MLA_NOPE - MLA_ROPE), w.dtype)], axis=-1)
    return w3.reshape(r, MLA_HEADS * MLA_HEAD_PAD)


def kernel(x, c, ctx, c_ctx, w_ada, b_ada, attn_norm, ffn_norm, w_in, mla_qa_norm, w_uq,
           mla_kva_norm, w_ukv, mla_q_gain, mla_knope_gain, mla_kpe_gain, diff_q_gain, diff_k_gain,
           diff_lambda, diff_subln, na_q_gain, na_k_gain, na_rpb, w_out, router_w, router_b,
           w_gu, b_gu, w_down, b_down):
    depth = w_in.shape[0]
    B, N, D = x.shape
    assert B == 1
    C = ctx.shape[1]
    T = N + C
    rows = N // GRID_W
    cos, sin = _rope_tables(N, C)
    silu = jnp.stack([jax.nn.silu(c[0]), jax.nn.silu(c_ctx)])
    silu_pad = jnp.concatenate([silu, jnp.zeros((14, D), F32)], axis=0).astype(BF16)
    mla_scale = (MLA_NOPE + MLA_ROPE) ** -0.5
    diff_scale = DIFF_QK ** -0.5
    na_scale = NA_DIM ** -0.5

    xl = x[0]
    xc = ctx[0]
    for l in range(depth):
        with_ctx_out = l < depth - 1
        lam_init = 0.8 - 0.6 * math.exp(-0.3 * l)
        mod = mm(silu_pad, w_ada[l], tn_target=512, name="ada")[:2] + b_ada[l]
        sh1, sc1, g1, sh2, sc2, g2 = jnp.split(mod[0], 6)
        sh1c, sc1c, g1c, sh2c, sc2c, g2c = jnp.split(mod[1], 6)

        h = _rmsnorm(xl, attn_norm[l]) * (1 + sc1) + sh1
        hc = _rmsnorm(xc, attn_norm[l]) * (1 + sc1c) + sh1c
        hh = jnp.concatenate([h, hc], axis=0).astype(BF16)
        proj = mm(hh, _pad_w_in(w_in[l]).astype(BF16), name="w_in")

        qa = _rmsnorm(proj[:, _P_QA:_P_KVA], mla_qa_norm[l]).astype(BF16)
        qf = mm(qa, _pad_w_uq(w_uq[l]).astype(BF16), name="w_uq").reshape(T, MLA_HEADS, MLA_HEAD_PAD)
        ss = jnp.sum(qf * qf, axis=-1, keepdims=True) / (MLA_NOPE + MLA_ROPE)
        qn = qf[..., :MLA_NOPE + MLA_ROPE] * lax.rsqrt(ss + EPS) * mla_q_gain[l]
        q_a = jnp.concatenate(
            [qn[..., :MLA_NOPE], _rope(qn[..., MLA_NOPE:], cos, sin),
             jnp.zeros((T, MLA_HEADS, MLA_HEAD_PAD - MLA_NOPE - MLA_ROPE), F32)], axis=-1)
        q_a = (q_a * mla_scale).astype(BF16).reshape(T, MLA_HEADS * MLA_HEAD_PAD)
        kva = _rmsnorm(proj[:, _P_KVA:_P_KPE], mla_kva_norm[l]).astype(BF16)
        kv = mm(kva, w_ukv[l].astype(BF16), name="w_ukv")
        kv3 = kv.reshape(T, MLA_HEADS, MLA_NOPE + MLA_V)
        k_nope = _rmsnorm(kv3[..., :MLA_NOPE], mla_knope_gain[l])
        k_pe = _rope(_rmsnorm(proj[:, _P_KPE:_P_KPE + MLA_ROPE], mla_kpe_gain[l]), cos, sin)
        k_a = jnp.concatenate(
            [k_nope, jnp.broadcast_to(k_pe[:, None], (T, MLA_HEADS, MLA_ROPE)),
             jnp.zeros((T, MLA_HEADS, MLA_HEAD_PAD - MLA_NOPE - MLA_ROPE), F32)], axis=-1)
        k_a = k_a.astype(BF16).reshape(T, MLA_HEADS * MLA_HEAD_PAD)
        kv_b = kv.astype(BF16)

        q_b = _rope(_rmsnorm(proj[:, _P_DQ:_P_DK].reshape(T, DIFF_HEADS, 2, DIFF_QK), diff_q_gain[l]), cos, sin)
        q_b = (q_b * diff_scale).astype(BF16).reshape(T, DIFF_HEADS * 2 * DIFF_QK)
        k_b = _rope(_rmsnorm(proj[:, _P_DK:_P_DV].reshape(T, DIFF_HEADS, 2, DIFF_QK), diff_k_gain[l]), cos, sin)
        k_b = k_b.astype(BF16).reshape(T, DIFF_HEADS * 2 * DIFF_QK)
        v_b = proj[:, _P_DV:_P_NQ].astype(BF16)
        lf = diff_lambda[l].astype(F32)
        lam = jnp.exp(jnp.sum(lf[0] * lf[1])) - jnp.exp(jnp.sum(lf[2] * lf[3])) + lam_init
        lam_row = jnp.full((1, LANES), lam, F32)
        subln_row = diff_subln[l].astype(F32).reshape(1, DIFF_V)
        diff_args = (lam_row, subln_row, 1.0 - lam_init)

        q_c = (_rmsnorm(proj[:, _P_NQ:_P_NK].reshape(T, NA_HEADS, NA_DIM), na_q_gain[l]) * na_scale)
        q_c = q_c.astype(BF16).reshape(T, NA_HEADS * NA_DIM)
        k_c = _rmsnorm(proj[:, _P_NK:_P_NV].reshape(T, NA_HEADS, NA_DIM), na_k_gain[l])
        k_c = k_c.astype(BF16).reshape(T, NA_HEADS * NA_DIM)
        v_c = proj[:, _P_NV:_P_END].astype(BF16)

        o_a = flash(q_a, k_a, kv_b, heads=MLA_HEADS, dk=MLA_HEAD_PAD, v_col=lambda hd: 2 * hd + 1,
                    n_q=N, q_start=0, n_k=T, k_start=0, tq_target=512, tk_target=1280, name="mla")
        o_b = flash(q_b, k_b, v_b, heads=DIFF_HEADS, dk=2 * DIFF_QK, v_col=lambda hd: hd,
                    n_q=N, q_start=0, n_k=T, k_start=0, tq_target=256, tk_target=1280,
                    diff_args=diff_args, name="diff")
        o_c = neighborhood(q_c, k_c, v_c, _na_bias_tables(na_rpb[l], rows), N)
        o_cat = jnp.concatenate([o_a, o_b, o_c], axis=-1)
        w_out_b = w_out[l].astype(BF16)
        o = mm(o_cat, w_out_b, tm_target=1024, name="w_out")
        xl = xl + g1 * o
        h2 = _rmsnorm(xl, ffn_norm[l]) * (1 + sc2) + sh2

        wg = w_gu[l][:, :, 0::2].astype(BF16)
        wl = w_gu[l][:, :, 1::2].astype(BF16)
        bg = b_gu[l][:, None, 0::2].astype(F32)
        bl = b_gu[l][:, None, 1::2].astype(F32)
        wd = w_down[l].astype(BF16)
        bd = b_down[l][:, None, :].astype(F32)
        rw = jnp.concatenate([router_w[l].astype(F32), jnp.zeros((D, LANES - N_EXPERTS), F32)], axis=1)

        if with_ctx_out:
            oc_a = flash(q_a, k_a, kv_b, heads=MLA_HEADS, dk=MLA_HEAD_PAD, v_col=lambda hd: 2 * hd + 1,
                         n_q=C, q_start=N, n_k=C, k_start=N, tq_target=256, tk_target=256, name="mla_ctx")
            oc_b = flash(q_b, k_b, v_b, heads=DIFF_HEADS, dk=2 * DIFF_QK, v_col=lambda hd: hd,
                         n_q=C, q_start=N, n_k=C, k_start=N, tq_target=256, tk_target=256,
                         diff_args=diff_args, name="diff_ctx")
            oc_c = flash(q_c, k_c, v_c, heads=NA_HEADS, dk=NA_DIM, v_col=lambda hd: hd,
                         n_q=C, q_start=N, n_k=C, k_start=N, tq_target=256, tk_target=256, name="na_ctx")
            oc = mm(jnp.concatenate([oc_a, oc_b, oc_c], axis=-1), w_out_b, name="w_out_ctx")
            xc = xc + g1c * oc
            hc2 = _rmsnorm(xc, ffn_norm[l]) * (1 + sc2c) + sh2c
            y = moe_ffn(jnp.concatenate([h2, hc2], axis=0), rw, router_b[l], wg, wl, bg, bl, wd, bd)
            xl = xl + g2 * y[:N]
            xc = xc + g2c * y[N:]
        else:
            xl = xl + g2 * moe_ffn(h2, rw, router_b[l], wg, wl, bg, bl, wd, bd)
    return xl[None]
```

```python
import functools
import math

import numpy as np
import jax
import jax.numpy as jnp
from jax import lax
from jax.experimental import pallas as pl
from jax.experimental.pallas import tpu as pltpu

F32 = jnp.float32
BF16 = jnp.bfloat16

GRID_W = 64
EPS = 1e-6
ROPE_THETA = 10000.0
ROT_DIM = 64
MLA_HEADS = 8
MLA_NOPE = 128
MLA_ROPE = 64
MLA_V = 128
MLA_Q_RANK = 512
MLA_KV_RANK = 256
MLA_HEAD_PAD = 256
DIFF_HEADS = 4
DIFF_QK = 64
DIFF_V = 128
NA_HEADS = 4
NA_DIM = 128
NA_KH = 8
NA_KW = 16
NA_ROWS_PER_GROUP = 4
N_EXPERTS = 32
TOP_K = 4
D_EXPERT = 768
SWIGLU_ALPHA = 1.702
SWIGLU_LIMIT = 7.0
MOE_BLOCK = 256
NEG_BIG = -1e30
LOG2E = math.log2(math.e)

LANES = 128
MXU_DIM = 256
VMEM_LIMIT = 56 * 1024 * 1024

NT_DIMS = (((1,), (1,)), ((), ()))


def _pick_tile(m, target, mult=8):
    best = None
    for t in range(mult, min(m, target) + 1, mult):
        if m % t == 0:
            best = t
    assert best is not None, (m, target)
    return best


def _mm_kernel(a_ref, b_ref, o_ref):
    a = a_ref[...].astype(BF16)
    b = b_ref[...].astype(BF16)
    o_ref[...] = jnp.dot(a, b, preferred_element_type=F32).astype(o_ref.dtype)


def mm(a, b, out_dtype=F32, tm_target=1280, tn_target=512, name="mm"):
    M, K = a.shape
    K2, N = b.shape
    assert K == K2
    tm = _pick_tile(M, tm_target)
    tn = _pick_tile(N, tn_target, LANES)
    return pl.pallas_call(
        _mm_kernel,
        grid=(M // tm, N // tn),
        in_specs=[pl.BlockSpec((tm, K), lambda i, j: (i, 0)),
                  pl.BlockSpec((K, tn), lambda i, j: (0, j))],
        out_specs=pl.BlockSpec((tm, tn), lambda i, j: (i, j)),
        out_shape=jax.ShapeDtypeStruct((M, N), out_dtype),
        compiler_params=pltpu.CompilerParams(
            dimension_semantics=("parallel", "arbitrary"), vmem_limit_bytes=VMEM_LIMIT),
        name=name,
    )(a, b)


def _mm_f32_kernel(a_ref, b_ref, o_ref):
    o_ref[...] = jnp.dot(a_ref[...], b_ref[...], preferred_element_type=F32,
                         precision=lax.Precision.HIGHEST)


def mm_f32(a, b, tm_target=512, name="mm_f32"):
    M, K = a.shape
    _, N = b.shape
    tm = _pick_tile(M, tm_target)
    return pl.pallas_call(
        _mm_f32_kernel,
        grid=(M // tm,),
        in_specs=[pl.BlockSpec((tm, K), lambda i: (i, 0)),
                  pl.BlockSpec((K, N), lambda i: (0, 0))],
        out_specs=pl.BlockSpec((tm, N), lambda i: (i, 0)),
        out_shape=jax.ShapeDtypeStruct((M, N), F32),
        compiler_params=pltpu.CompilerParams(
            dimension_semantics=("parallel",), vmem_limit_bytes=VMEM_LIMIT),
        name=name,
    )(a, b)


def _flash_kernel(*refs, tk, n_chunks, diff, lam_scale):
    if diff:
        q_ref, k_ref, v_ref, lam_ref, g_ref, o_ref, q_scr, s_scr, acc_scr = refs
    else:
        q_ref, k_ref, v_ref, o_ref, q_scr, s_scr, acc_scr = refs
    q = q_ref[...]
    tq = q.shape[0]
    if diff:
        lane = lax.broadcasted_iota(jnp.int32, q.shape, 1)
        zero = jnp.zeros_like(q)
        q_scr[:tq, :] = jnp.where(lane < DIFF_QK, q, zero)
        q_scr[tq:, :] = jnp.where(lane >= DIFF_QK, q, zero)
    else:
        q_scr[...] = q
    rows = q_scr.shape[0]
    acc_scr[...] = jnp.zeros_like(acc_scr)

    def qk(c, slot):
        start = pl.multiple_of(c * tk, tk)
        s_scr[slot] = lax.dot_general(q_scr[...], k_ref[pl.ds(start, tk), :], NT_DIMS,
                                      preferred_element_type=F32)

    def consume(c, slot, m):
        s = s_scr[slot]
        m_new = jnp.maximum(m, jnp.max(s, axis=-1, keepdims=True))
        alpha = jnp.exp2(m - m_new)
        p = jnp.exp2(s - m_new).astype(BF16)
        start = pl.multiple_of(c * tk, tk)
        acc_scr[...] = alpha * acc_scr[...] + jnp.dot(
            p, v_ref[pl.ds(start, tk), :], preferred_element_type=F32)
        return m_new

    def pair(i, m):
        c = 2 * i
        qk(c + 1, 1)
        m = consume(c, 0, m)
        qk(c + 2, 0)
        return consume(c + 1, 1, m)

    m = jnp.full((rows, 1), -jnp.inf, F32)
    qk(0, 0)
    n_pairs = (n_chunks - 1) // 2
    if n_pairs > 0:
        m = lax.fori_loop(0, n_pairs, pair, m, unroll=True)
    if n_chunks % 2 == 1:
        m = consume(n_chunks - 1, 0, m)
    else:
        qk(n_chunks - 1, 1)
        m = consume(n_chunks - 2, 0, m)
        m = consume(n_chunks - 1, 1, m)
    acc = acc_scr[...]
    o = acc[:, :LANES] / acc[:, LANES:LANES + 1]
    if diff:
        o = o[:tq] - lam_ref[...] * o[tq:]
        ms = jnp.mean(o * o, axis=-1, keepdims=True)
        o = o * lax.rsqrt(ms + EPS) * g_ref[...] * lam_scale
    o_ref[...] = o.astype(o_ref.dtype)


def flash(q, k, v, *, heads, dk, n_q, q_start, n_k, k_start, tq_target, tk_target,
          diff_args=None, name="flash"):
    tq = _pick_tile(math.gcd(n_q, q_start) if q_start else n_q, tq_target)
    tk = _pick_tile(n_k, tk_target, LANES)
    assert k_start % n_k == 0
    q_blk0 = q_start // tq
    k_blk0 = k_start // n_k
    vw = 2 * LANES
    in_specs = [pl.BlockSpec((tq, dk), lambda h, i: (q_blk0 + i, h)),
                pl.BlockSpec((n_k, dk), lambda h, i: (k_blk0, h)),
                pl.BlockSpec((n_k, vw), lambda h, i: (k_blk0, h))]
    args = [q, k, v]
    diff = diff_args is not None
    lam_scale = 1.0
    if diff:
        lam_row, gain_row, lam_scale = diff_args
        in_specs += [pl.BlockSpec((1, LANES), lambda h, i: (0, 0)),
                     pl.BlockSpec((1, LANES), lambda h, i: (0, 0))]
        args += [lam_row, gain_row]
    rows = 2 * tq if diff else tq
    kern = functools.partial(_flash_kernel, tk=tk, n_chunks=n_k // tk, diff=diff,
                             lam_scale=lam_scale)
    return pl.pallas_call(
        kern,
        grid=(heads, n_q // tq),
        in_specs=in_specs,
        out_specs=pl.BlockSpec((tq, LANES), lambda h, i: (i, h)),
        out_shape=jax.ShapeDtypeStruct((n_q, heads * LANES), BF16),
        scratch_shapes=[pltpu.VMEM((rows, dk), BF16),
                        pltpu.VMEM((2, rows, tk), F32),
                        pltpu.VMEM((rows, vw), F32)],
        compiler_params=pltpu.CompilerParams(
            dimension_semantics=("parallel", "arbitrary"), vmem_limit_bytes=VMEM_LIMIT),
        name=name,
    )(*args)


def _with_ones_column(v3):
    T, H, d = v3.shape
    pad = jnp.zeros((T, H, 2 * LANES - d), BF16).at[:, :, 0].set(1.0)
    return jnp.concatenate([v3.astype(BF16), pad], axis=-1).reshape(T, H * 2 * LANES)


def _na_kernel(q_ref, k0_ref, k1_ref, k2_ref, v0_ref, v1_ref, v2_ref, kc_ref, vc_ref, bias_ref,
               o_ref):
    k_refs = (k0_ref, k1_ref, k2_ref, kc_ref)
    v_refs = (v0_ref, v1_ref, v2_ref, vc_ref)
    blk = k0_ref.shape[0]
    for h in range(NA_HEADS):
        cols = slice(h * NA_DIM, (h + 1) * NA_DIM)
        q = q_ref[:, cols]
        s_parts = []
        for b in range(4):
            s = lax.dot_general(q, k_refs[b][:, cols], NT_DIMS, preferred_element_type=F32)
            if b < 3:
                s = s + bias_ref[0, h, :, b * blk:(b + 1) * blk]
            s_parts.append(s)
        m = s_parts[3].max(axis=-1, keepdims=True)
        for b in range(3):
            m = jnp.maximum(m, s_parts[b].max(axis=-1, keepdims=True))
        l = jnp.zeros_like(m)
        acc = jnp.zeros((q.shape[0], NA_DIM), F32)
        for b in range(4):
            p = jnp.exp(s_parts[b] - m)
            l = l + p.sum(axis=-1, keepdims=True)
            acc = acc + jnp.dot(p.astype(BF16), v_refs[b][:, cols], preferred_element_type=F32)
        o_ref[:, cols] = (acc / l).astype(o_ref.dtype)


def _na_bias_tables(rpb, rows):
    R = NA_ROWS_PER_GROUP
    G = rows // R
    assert rows % R == 0 and G >= 3 and rows >= NA_KH and NA_KH == 2 * R
    tabs = []
    for g in (0, 1, G - 1):
        j = np.arange(R)[:, None, None, None, None]
        qc = np.arange(GRID_W)[None, :, None, None, None]
        b = np.arange(3)[None, None, :, None, None]
        kr = np.arange(R)[None, None, None, :, None]
        kc = np.arange(GRID_W)[None, None, None, None, :]
        r = R * g + j
        rs = np.clip(r - NA_KH // 2, 0, rows - NA_KH)
        keyrow = R * (g - 1 + b) + kr
        wc = np.clip(qc - NA_KW // 2, 0, GRID_W - NA_KW)
        valid = (keyrow >= rs) & (keyrow < rs + NA_KH) & (kc >= wc) & (kc < wc + NA_KW)
        row_rel = np.clip(keyrow - r + NA_KH - 1, 0, 2 * NA_KH - 2)
        col_rel = np.clip(kc - qc + NA_KW - 1, 0, 2 * NA_KW - 2)
        shape = (R, GRID_W, 3, R, GRID_W)
        valid = np.broadcast_to(valid, shape).reshape(R * GRID_W, 3 * R * GRID_W)
        row_rel = np.broadcast_to(row_rel, shape).reshape(R * GRID_W, 3 * R * GRID_W)
        col_rel = np.broadcast_to(col_rel, shape).reshape(R * GRID_W, 3 * R * GRID_W)
        bias = rpb.astype(F32)[:, row_rel, col_rel]
        tabs.append(jnp.where(valid[None], bias, NEG_BIG))
    return jnp.stack(tabs)


def neighborhood(q, k, v, bias_tabs, n_lat):
    blk = NA_ROWS_PER_GROUP * GRID_W
    G = n_lat // blk
    n_ctx = q.shape[0] - n_lat
    assert n_ctx == blk, "context length must equal one key block"
    W = NA_HEADS * NA_DIM

    def kv_spec(off):
        return pl.BlockSpec((blk, W), lambda g: (jnp.clip(g + off, 0, G - 1), 0))

    ctx_spec = pl.BlockSpec((blk, W), lambda g: (G, 0))
    bias_spec = pl.BlockSpec(
        (1, NA_HEADS, blk, 3 * blk),
        lambda g: (jnp.where(g == 0, 0, jnp.where(g == G - 1, 2, 1)), 0, 0, 0))
    return pl.pallas_call(
        _na_kernel,
        grid=(G,),
        in_specs=[pl.BlockSpec((blk, W), lambda g: (g, 0)),
                  kv_spec(-1), kv_spec(0), kv_spec(1),
                  kv_spec(-1), kv_spec(0), kv_spec(1),
                  ctx_spec, ctx_spec, bias_spec],
        out_specs=pl.BlockSpec((blk, W), lambda g: (g, 0)),
        out_shape=jax.ShapeDtypeStruct((n_lat, W), BF16),
        compiler_params=pltpu.CompilerParams(
            dimension_semantics=("parallel",), vmem_limit_bytes=VMEM_LIMIT),
        name="na",
    )(q, k, k, k, v, v, v, k, v, bias_tabs)


def _deinterleave_matrix():
    p = np.zeros((MXU_DIM, MXU_DIM), np.float32)
    i = np.arange(MXU_DIM // 2)
    p[2 * i, i] = 1.0
    p[2 * i + 1, MXU_DIM // 2 + i] = 1.0
    return jnp.asarray(p, BF16)


def _deint_kernel(w_ref, p_ref, o_ref):
    w = w_ref[0].astype(BF16)
    o_ref[0] = jnp.dot(w, p_ref[...], preferred_element_type=F32).astype(BF16)


def deinterleave_gu(w_gu):
    E, D, F2 = w_gu.shape
    return pl.pallas_call(
        _deint_kernel,
        grid=(E, F2 // MXU_DIM),
        in_specs=[pl.BlockSpec((1, D, MXU_DIM), lambda e, j: (e, 0, j)),
                  pl.BlockSpec((MXU_DIM, MXU_DIM), lambda e, j: (0, 0))],
        out_specs=pl.BlockSpec((1, D, MXU_DIM), lambda e, j: (e, 0, j)),
        out_shape=jax.ShapeDtypeStruct((E, D, F2), BF16),
        compiler_params=pltpu.CompilerParams(
            dimension_semantics=("parallel", "parallel"), vmem_limit_bytes=VMEM_LIMIT),
        name="deint_gu",
    )(w_gu, _deinterleave_matrix())


def _moe_kernel(blk_e_ref, n_used_ref, x_ref, wgu_ref, bgu_ref, wd_ref, bd_ref, o_ref):
    i = pl.program_id(0)

    @pl.when(i < n_used_ref[0])
    def _():
        x = x_ref[...]
        gu = jnp.dot(x, wgu_ref[0], preferred_element_type=F32) + bgu_ref[0]
        acts = []
        for c in range(gu.shape[1] // MXU_DIM):
            g = gu[:, c * MXU_DIM:c * MXU_DIM + LANES]
            u = gu[:, c * MXU_DIM + LANES:(c + 1) * MXU_DIM]
            glu = jnp.minimum(g, SWIGLU_LIMIT)
            lin = jnp.clip(u, -SWIGLU_LIMIT, SWIGLU_LIMIT)
            acts.append((glu * jax.nn.sigmoid(SWIGLU_ALPHA * glu) * (lin + 1.0)).astype(BF16))
        act = jnp.concatenate(acts, axis=1)
        y = jnp.dot(act, wd_ref[0], preferred_element_type=F32) + bd_ref[0]
        o_ref[...] = y.astype(o_ref.dtype)

    @pl.when(i >= n_used_ref[0])
    def _():
        o_ref[...] = jnp.zeros_like(o_ref)


def moe_experts(xb, blk_e, n_used, wgu, bgu, wd, bd):
    cap, D = xb.shape
    n_blocks = cap // MOE_BLOCK
    F2 = wgu.shape[-1]
    F = wd.shape[1]
    grid_spec = pltpu.PrefetchScalarGridSpec(
        num_scalar_prefetch=2,
        grid=(n_blocks,),
        in_specs=[pl.BlockSpec((MOE_BLOCK, D), lambda i, e, n: (i, 0)),
                  pl.BlockSpec((1, D, F2), lambda i, e, n: (e[i], 0, 0)),
                  pl.BlockSpec((1, 1, F2), lambda i, e, n: (e[i], 0, 0)),
                  pl.BlockSpec((1, F, D), lambda i, e, n: (e[i], 0, 0)),
                  pl.BlockSpec((1, 1, D), lambda i, e, n: (e[i], 0, 0))],
        out_specs=pl.BlockSpec((MOE_BLOCK, D), lambda i, e, n: (i, 0)),
    )
    return pl.pallas_call(
        _moe_kernel,
        grid_spec=grid_spec,
        out_shape=jax.ShapeDtypeStruct((cap, D), BF16),
        compiler_params=pltpu.CompilerParams(
            dimension_semantics=("arbitrary",), vmem_limit_bytes=VMEM_LIMIT),
        name="moe_experts",
    )(blk_e, n_used, xb, wgu, bgu, wd, bd)


def moe_ffn(h, router_w_pad, router_b, wgu, bgu, wd, bd):
    n, D = h.shape
    logits = mm_f32(h, router_w_pad, name="router")[:, :N_EXPERTS] + router_b.astype(F32)
    top_v, top_e = lax.top_k(logits, TOP_K)
    gates = jax.nn.softmax(top_v, axis=-1)
    n_assign = n * TOP_K
    flat_e = top_e.reshape(-1).astype(jnp.int32)
    onehot = (flat_e[:, None] == jnp.arange(N_EXPERTS, dtype=jnp.int32)[None, :]).astype(jnp.int32)
    csum = jnp.cumsum(onehot, axis=0)
    counts = csum[-1]
    padded = (counts + MOE_BLOCK - 1) // MOE_BLOCK * MOE_BLOCK
    pad_end = jnp.cumsum(padded)
    pad_start = pad_end - padded
    pos = jnp.sum(onehot * (csum - 1 + pad_start[None, :]), axis=1)
    n_blocks = -(-n_assign // MOE_BLOCK) + N_EXPERTS
    cap = n_blocks * MOE_BLOCK
    row_tok = jnp.full((cap,), n, jnp.int32).at[pos].set(
        jnp.arange(n_assign, dtype=jnp.int32) // TOP_K, unique_indices=True)
    blk_e = jnp.minimum(
        jnp.searchsorted(pad_end, jnp.arange(n_blocks, dtype=jnp.int32) * MOE_BLOCK, side='right'),
        N_EXPERTS - 1).astype(jnp.int32)
    n_used = (pad_end[-1:] // MOE_BLOCK).astype(jnp.int32)
    xpad = jnp.concatenate([h.astype(BF16), jnp.zeros((1, D), BF16)], axis=0)
    xb = xpad[row_tok]
    yb = moe_experts(xb, blk_e, n_used, wgu, bgu, wd, bd)
    yg = lax.optimization_barrier(yb[pos])
    y = jnp.sum(yg.reshape(n, TOP_K, D).astype(F32) * gates[:, :, None], axis=1)
    return y


def _rmsnorm(x, g):
    xf = x.astype(F32)
    return xf * lax.rsqrt(jnp.mean(xf * xf, axis=-1, keepdims=True) + EPS) * g.astype(F32)


def _rope_tables(n_lat, n_ctx):
    quarter = ROT_DIM // 4
    inv = 1.0 / (ROPE_THETA ** (jnp.arange(quarter, dtype=F32) / quarter))
    t = jnp.arange(n_lat)
    row = (t // GRID_W).astype(F32)
    col = (t % GRID_W).astype(F32)
    ang = jnp.concatenate([row[:, None] * inv, col[:, None] * inv], axis=-1)
    ang = jnp.concatenate([ang, jnp.zeros((n_ctx, ROT_DIM // 2), F32)], axis=0)
    return jnp.cos(ang), jnp.sin(ang)


def _rope(x, cos, sin):
    half = x.shape[-1] // 2
    expand = (1,) * (x.ndim - 2)
    cos = cos.reshape((cos.shape[0],) + expand + (half,))
    sin = sin.reshape((sin.shape[0],) + expand + (half,))
    x1, x2 = x[..., :half], x[..., half:]
    return jnp.concatenate([x1 * cos - x2 * sin, x1 * sin + x2 * cos], axis=-1)


_P_QA, _P_KVA, _P_KPE, _P_DQ, _P_DK, _P_DV, _P_NQ, _P_NK, _P_NV, _P_END = (
    0, 512, 768, 1024, 1536, 2048, 2560, 3072, 3584, 4096)


def _pad_w_in(w):
    D = w.shape[0]
    a = MLA_Q_RANK + MLA_KV_RANK + MLA_ROPE
    return jnp.concatenate([w[:, :a], jnp.zeros((D, _P_DQ - a), w.dtype), w[:, a:]], axis=1)


def _pad_w_uq(w):
    r = w.shape[0]
    w3 = w.reshape(r, MLA_HEADS, MLA_NOPE + MLA_ROPE)
    w3 = jnp.concatenate(
        [w3, jnp.zeros((r, MLA_HEADS, MLA_HEAD_PAD - MLA_NOPE - MLA_ROPE), w.dtype)], axis=-1)
    return w3.reshape(r, MLA_HEADS * MLA_HEAD_PAD)


def kernel(x, c, ctx, c_ctx, w_ada, b_ada, attn_norm, ffn_norm, w_in, mla_qa_norm, w_uq,
           mla_kva_norm, w_ukv, mla_q_gain, mla_knope_gain, mla_kpe_gain, diff_q_gain, diff_k_gain,
           diff_lambda, diff_subln, na_q_gain, na_k_gain, na_rpb, w_out, router_w, router_b,
           w_gu, b_gu, w_down, b_down):
    depth = w_in.shape[0]
    B, N, D = x.shape
    assert B == 1
    C = ctx.shape[1]
    T = N + C
    rows = N // GRID_W
    cos, sin = _rope_tables(N, C)
    silu = jnp.stack([jax.nn.silu(c[0]), jax.nn.silu(c_ctx)])
    silu_pad = jnp.concatenate([silu, jnp.zeros((14, D), F32)], axis=0).astype(BF16)
    mla_scale = (MLA_NOPE + MLA_ROPE) ** -0.5 * LOG2E
    diff_scale = DIFF_QK ** -0.5 * LOG2E
    na_scale = NA_DIM ** -0.5
    n_groups = w_gu.shape[-1] // MXU_DIM

    xl = x[0]
    xc = ctx[0]
    for l in range(depth):
        with_ctx_out = l < depth - 1
        lam_init = 0.8 - 0.6 * math.exp(-0.3 * l)
        mod = mm(silu_pad, w_ada[l], tn_target=512, name="ada")[:2] + b_ada[l]
        sh1, sc1, g1, sh2, sc2, g2 = jnp.split(mod[0], 6)
        sh1c, sc1c, g1c, sh2c, sc2c, g2c = jnp.split(mod[1], 6)

        h = _rmsnorm(xl, attn_norm[l]) * (1 + sc1) + sh1
        hc = _rmsnorm(xc, attn_norm[l]) * (1 + sc1c) + sh1c
        hh = jnp.concatenate([h, hc], axis=0).astype(BF16)
        proj = mm(hh, _pad_w_in(w_in[l]).astype(BF16), name="w_in")

        qa = _rmsnorm(proj[:, _P_QA:_P_KVA], mla_qa_norm[l]).astype(BF16)
        qf = mm(qa, _pad_w_uq(w_uq[l]).astype(BF16), name="w_uq").reshape(T, MLA_HEADS, MLA_HEAD_PAD)
        ss = jnp.sum(qf * qf, axis=-1, keepdims=True) / (MLA_NOPE + MLA_ROPE)
        qn = qf[..., :MLA_NOPE + MLA_ROPE] * lax.rsqrt(ss + EPS) * mla_q_gain[l]
        q_a = jnp.concatenate(
            [qn[..., :MLA_NOPE], _rope(qn[..., MLA_NOPE:], cos, sin),
             jnp.zeros((T, MLA_HEADS, MLA_HEAD_PAD - MLA_NOPE - MLA_ROPE), F32)], axis=-1)
        q_a = (q_a * mla_scale).astype(BF16).reshape(T, MLA_HEADS * MLA_HEAD_PAD)
        kva = _rmsnorm(proj[:, _P_KVA:_P_KPE], mla_kva_norm[l]).astype(BF16)
        kv = mm(kva, w_ukv[l].astype(BF16), name="w_ukv")
        kv3 = kv.reshape(T, MLA_HEADS, MLA_NOPE + MLA_V)
        k_nope = _rmsnorm(kv3[..., :MLA_NOPE], mla_knope_gain[l])
        k_pe = _rope(_rmsnorm(proj[:, _P_KPE:_P_KPE + MLA_ROPE], mla_kpe_gain[l]), cos, sin)
        k_a = jnp.concatenate(
            [k_nope, jnp.broadcast_to(k_pe[:, None], (T, MLA_HEADS, MLA_ROPE)),
             jnp.zeros((T, MLA_HEADS, MLA_HEAD_PAD - MLA_NOPE - MLA_ROPE), F32)], axis=-1)
        k_a = k_a.astype(BF16).reshape(T, MLA_HEADS * MLA_HEAD_PAD)
        v_a = _with_ones_column(kv3[..., MLA_NOPE:])

        q_b = _rope(_rmsnorm(proj[:, _P_DQ:_P_DK].reshape(T, DIFF_HEADS, 2, DIFF_QK), diff_q_gain[l]), cos, sin)
        q_b = (q_b * diff_scale).astype(BF16).reshape(T, DIFF_HEADS * 2 * DIFF_QK)
        k_b = _rope(_rmsnorm(proj[:, _P_DK:_P_DV].reshape(T, DIFF_HEADS, 2, DIFF_QK), diff_k_gain[l]), cos, sin)
        k_b = k_b.astype(BF16).reshape(T, DIFF_HEADS * 2 * DIFF_QK)
        v_b = _with_ones_column(proj[:, _P_DV:_P_NQ].reshape(T, DIFF_HEADS, DIFF_V))
        lf = diff_lambda[l].astype(F32)
        lam = jnp.exp(jnp.sum(lf[0] * lf[1])) - jnp.exp(jnp.sum(lf[2] * lf[3])) + lam_init
        lam_row = jnp.full((1, LANES), lam, F32)
        subln_row = diff_subln[l].astype(F32).reshape(1, DIFF_V)
        diff_args = (lam_row, subln_row, 1.0 - lam_init)

        q_cf = _rmsnorm(proj[:, _P_NQ:_P_NK].reshape(T, NA_HEADS, NA_DIM), na_q_gain[l]) * na_scale
        q_c = q_cf.astype(BF16).reshape(T, NA_HEADS * NA_DIM)
        k_c = _rmsnorm(proj[:, _P_NK:_P_NV].reshape(T, NA_HEADS, NA_DIM), na_k_gain[l])
        k_c = k_c.astype(BF16).reshape(T, NA_HEADS * NA_DIM)
        v_c = proj[:, _P_NV:_P_END].astype(BF16)

        o_a = flash(q_a, k_a, v_a, heads=MLA_HEADS, dk=MLA_HEAD_PAD,
                    n_q=N, q_start=0, n_k=T, k_start=0, tq_target=512, tk_target=1280, name="mla")
        o_b = flash(q_b, k_b, v_b, heads=DIFF_HEADS, dk=2 * DIFF_QK,
                    n_q=N, q_start=0, n_k=T, k_start=0, tq_target=256, tk_target=1280,
                    diff_args=diff_args, name="diff")
        o_c = neighborhood(q_c, k_c, v_c, _na_bias_tables(na_rpb[l], rows), N)
        o_cat = jnp.concatenate([o_a, o_b, o_c], axis=-1)
        w_out_b = w_out[l].astype(BF16)
        o = mm(o_cat, w_out_b, tm_target=1024, name="w_out")
        xl = xl + g1 * o
        h2 = _rmsnorm(xl, ffn_norm[l]) * (1 + sc2) + sh2

        wgu = deinterleave_gu(w_gu[l])
        bgu = b_gu[l].astype(F32).reshape(N_EXPERTS, n_groups, LANES, 2)
        bgu = bgu.transpose(0, 1, 3, 2).reshape(N_EXPERTS, 1, n_groups * MXU_DIM)
        wd = w_down[l].astype(BF16)
        bd = b_down[l][:, None, :].astype(F32)
        rw = jnp.concatenate([router_w[l].astype(F32), jnp.zeros((D, LANES - N_EXPERTS), F32)], axis=1)

        if with_ctx_out:
            q_cc = (q_cf[N:] * LOG2E).astype(BF16).reshape(C, NA_HEADS * NA_DIM)
            v_cc = _with_ones_column(proj[N:, _P_NV:_P_END].reshape(C, NA_HEADS, NA_DIM))
            oc_a = flash(q_a, k_a, v_a, heads=MLA_HEADS, dk=MLA_HEAD_PAD,
                         n_q=C, q_start=N, n_k=C, k_start=N, tq_target=256, tk_target=256, name="mla_ctx")
            oc_b = flash(q_b, k_b, v_b, heads=DIFF_HEADS, dk=2 * DIFF_QK,
                         n_q=C, q_start=N, n_k=C, k_start=N, tq_target=256, tk_target=256,
                         diff_args=diff_args, name="diff_ctx")
            oc_c = flash(q_cc, k_c[N:], v_cc, heads=NA_HEADS, dk=NA_DIM,
                         n_q=C, q_start=0, n_k=C, k_start=0, tq_target=256, tk_target=256, name="na_ctx")
            oc = mm(jnp.concatenate([oc_a, oc_b, oc_c], axis=-1), w_out_b, name="w_out_ctx")
            xc = xc + g1c * oc
            hc2 = _rmsnorm(xc, ffn_norm[l]) * (1 + sc2c) + sh2c
            y = moe_ffn(jnp.concatenate([h2, hc2], axis=0), rw, router_b[l], wgu, bgu, wd, bd)
            xl = xl + g2 * y[:N]
            xc = xc + g2c * y[N:]
        else:
            xl = xl + g2 * moe_ffn(h2, rw, router_b[l], wgu, bgu, wd, bd)
    return xl[None]
```

```python
import functools
import math

import numpy as np
import jax
import jax.numpy as jnp
from jax import lax
from jax.experimental import pallas as pl
from jax.experimental.pallas import tpu as pltpu

F32 = jnp.float32
BF16 = jnp.bfloat16

GRID_W = 64
EPS = 1e-6
ROPE_THETA = 10000.0
ROT_DIM = 64
MLA_HEADS = 8
MLA_NOPE = 128
MLA_ROPE = 64
MLA_V = 128
MLA_Q_RANK = 512
MLA_KV_RANK = 256
MLA_HEAD_PAD = 256
DIFF_HEADS = 4
DIFF_QK = 64
DIFF_V = 128
NA_HEADS = 4
NA_DIM = 128
NA_KH = 8
NA_KW = 16
NA_ROWS_PER_GROUP = 4
N_EXPERTS = 32
TOP_K = 4
D_EXPERT = 768
SWIGLU_ALPHA = 1.702
SWIGLU_LIMIT = 7.0
MOE_BLOCK = 256
COMBINE_BLOCK = 128
ROW_BLOCK = 256
N_MOD = 6
NEG_BIG = -1e30
LOG2E = math.log2(math.e)

LANES = 128
MXU_DIM = 256
VMEM_LIMIT = 56 * 1024 * 1024

NT_DIMS = (((1,), (1,)), ((), ()))

_P_QA, _P_KVA, _P_KPE, _P_DQ, _P_DK, _P_DV, _P_NQ, _P_NK, _P_NV, _P_END = (
    0, 512, 768, 1024, 1536, 2048, 2560, 3072, 3584, 4096)


def _pick_tile(m, target, mult=8):
    best = None
    for t in range(mult, min(m, target) + 1, mult):
        if m % t == 0:
            best = t
    assert best is not None, (m, target)
    return best


def _params(*sem):
    return pltpu.CompilerParams(dimension_semantics=sem, vmem_limit_bytes=VMEM_LIMIT)


def _resident(block_shape, index_map):
    return pl.BlockSpec(block_shape, index_map, pipeline_mode=pl.Buffered(1))


def _mod_spec(layer, sel_fn, k):
    return lambda D: pl.BlockSpec((1, 1, D), lambda i: ((layer * 2 + sel_fn(i)) * N_MOD + k, 0, 0))


def _ada_kernel(a_ref, w_ref, o_ref):
    o_ref[0] = jnp.dot(a_ref[...], w_ref[0].astype(BF16), preferred_element_type=F32)


def ada_all(silu_pad, w_ada):
    L, D, N6 = w_ada.shape
    tn = _pick_tile(N6, 512, LANES)
    return pl.pallas_call(
        _ada_kernel,
        grid=(L, N6 // tn),
        in_specs=[pl.BlockSpec((16, D), lambda l, j: (0, 0)),
                  pl.BlockSpec((1, D, tn), lambda l, j: (l, 0, j))],
        out_specs=pl.BlockSpec((1, 16, tn), lambda l, j: (l, 0, j)),
        out_shape=jax.ShapeDtypeStruct((L, 16, N6), F32),
        compiler_params=_params("parallel", "parallel"),
        name="ada",
    )(silu_pad, w_ada)


def _norm_mm_kernel(x_ref, g_ref, sc_ref, sh_ref, w_ref, o_ref):
    x = x_ref[...]
    ms = jnp.mean(x * x, axis=-1, keepdims=True)
    h = x * lax.rsqrt(ms + EPS) * g_ref[0] * (1.0 + sc_ref[0]) + sh_ref[0]
    o_ref[...] = jnp.dot(h.astype(BF16), w_ref[0], preferred_element_type=F32)


def in_proj(xa, norm_g, mods, w_in_p, layer, n_lat):
    T, D = xa.shape
    P = w_in_p.shape[-1]
    nlb = n_lat // ROW_BLOCK
    sel = lambda i: jnp.where(i >= nlb, 1, 0)
    return pl.pallas_call(
        _norm_mm_kernel,
        grid=(T // ROW_BLOCK,),
        in_specs=[pl.BlockSpec((ROW_BLOCK, D), lambda i: (i, 0)),
                  pl.BlockSpec((1, 1, D), lambda i: (layer, 0, 0)),
                  _mod_spec(layer, sel, 1)(D),
                  _mod_spec(layer, sel, 0)(D),
                  _resident((1, D, P), lambda i: (layer, 0, 0))],
        out_specs=pl.BlockSpec((ROW_BLOCK, P), lambda i: (i, 0)),
        out_shape=jax.ShapeDtypeStruct((T, P), F32),
        compiler_params=_params("parallel"),
        name="w_in",
    )(xa, norm_g, mods, mods, w_in_p)


def _post_proj_kernel(p_ref, cos_ref, sin_ref, gqa_ref, gkva_ref, gq_ref, gkn_ref, gkpe_ref,
                      gdq_ref, gdk_ref, gnq_ref, gnk_ref, wuq_ref, wukv_ref,
                      qa_o, ka_o, va_o, qb_o, kb_o, vb_o, qc_o, kc_o, vc_o,
                      *, mla_scale, diff_scale, na_scale):
    tm = p_ref.shape[0]
    cos_t = cos_ref[...]
    sin_t = sin_ref[...]
    lane = lax.broadcasted_iota(jnp.int32, (tm, LANES), 1)
    lo = lane < ROT_DIM
    first_half = (lane & (ROT_DIM - 1)) < ROT_DIM // 2
    ones_col = jnp.where(lane == 0, 1.0, 0.0).astype(BF16)

    def rope(x):
        sw = jnp.where(first_half, pltpu.roll(x, LANES - ROT_DIM // 2, 1),
                       pltpu.roll(x, ROT_DIM // 2, 1))
        return x * cos_t + sw * sin_t

    def rinv(x, n):
        return lax.rsqrt(jnp.sum(x * x, axis=-1, keepdims=True) * (1.0 / n) + EPS)

    qa = p_ref[:, _P_QA:_P_KVA]
    qa_n = (qa * rinv(qa, MLA_Q_RANK) * gqa_ref[0]).astype(BF16)
    qf = jnp.dot(qa_n, wuq_ref[0], preferred_element_type=F32)
    g_q = gq_ref[0]
    for h in range(MLA_HEADS):
        c0 = h * MLA_HEAD_PAD
        blk = qf[:, c0:c0 + MLA_HEAD_PAD]
        r = rinv(blk, MLA_NOPE + MLA_ROPE) * mla_scale
        qa_o[:, c0:c0 + LANES] = (blk[:, :LANES] * r * g_q[:, :LANES]).astype(BF16)
        qa_o[:, c0 + LANES:c0 + 2 * LANES] = rope(blk[:, LANES:] * r * g_q[:, LANES:]).astype(BF16)

    kva = p_ref[:, _P_KVA:_P_KPE]
    kva_n = (kva * rinv(kva, MLA_KV_RANK) * gkva_ref[0]).astype(BF16)
    kv = jnp.dot(kva_n, wukv_ref[0], preferred_element_type=F32)
    kpe = p_ref[:, _P_KPE:_P_KPE + LANES]
    kpe_r = rope(kpe * rinv(kpe, MLA_ROPE) * gkpe_ref[0]).astype(BF16)
    for h in range(MLA_HEADS):
        c0 = h * MLA_HEAD_PAD
        kn = kv[:, c0:c0 + LANES]
        ka_o[:, c0:c0 + LANES] = (kn * rinv(kn, MLA_NOPE) * gkn_ref[0]).astype(BF16)
        ka_o[:, c0 + LANES:c0 + 2 * LANES] = kpe_r
        va_o[:, c0:c0 + LANES] = kv[:, c0 + LANES:c0 + 2 * LANES].astype(BF16)
        va_o[:, c0 + LANES:c0 + 2 * LANES] = ones_col

    def subhead_norm_rope(x, g_ref, scale):
        sq = x * x
        s_lo = jnp.sum(jnp.where(lo, sq, 0.0), axis=-1, keepdims=True)
        s_hi = jnp.sum(jnp.where(lo, 0.0, sq), axis=-1, keepdims=True)
        r = jnp.where(lo, lax.rsqrt(s_lo * (1.0 / DIFF_QK) + EPS),
                      lax.rsqrt(s_hi * (1.0 / DIFF_QK) + EPS))
        return (rope(x * r * g_ref[0]) * scale).astype(BF16)

    for h in range(DIFF_HEADS):
        c0 = h * LANES
        qb_o[:, c0:c0 + LANES] = subhead_norm_rope(p_ref[:, _P_DQ + c0:_P_DQ + c0 + LANES], gdq_ref, diff_scale)
        kb_o[:, c0:c0 + LANES] = subhead_norm_rope(p_ref[:, _P_DK + c0:_P_DK + c0 + LANES], gdk_ref, 1.0)
        vb_o[:, 2 * c0:2 * c0 + LANES] = p_ref[:, _P_DV + c0:_P_DV + c0 + LANES].astype(BF16)
        vb_o[:, 2 * c0 + LANES:2 * c0 + 2 * LANES] = ones_col

    for h in range(NA_HEADS):
        c0 = h * NA_DIM
        x = p_ref[:, _P_NQ + c0:_P_NQ + c0 + NA_DIM]
        qc_o[:, c0:c0 + NA_DIM] = (x * rinv(x, NA_DIM) * gnq_ref[0] * na_scale).astype(BF16)
        x = p_ref[:, _P_NK + c0:_P_NK + c0 + NA_DIM]
        kc_o[:, c0:c0 + NA_DIM] = (x * rinv(x, NA_DIM) * gnk_ref[0]).astype(BF16)
    vc_o[...] = p_ref[:, _P_NV:_P_END].astype(BF16)


def post_proj(proj, cos_t, sin_t, gains, w_uq_p, w_ukv_b, layer):
    T, P = proj.shape
    tm = ROW_BLOCK
    row = lambda w: pl.BlockSpec((tm, w), lambda i: (i, 0))
    gain_specs = [pl.BlockSpec((1, 1, g.shape[-1]), lambda i: (layer, 0, 0)) for g in gains]
    widths = (MLA_HEADS * MLA_HEAD_PAD, MLA_HEADS * MLA_HEAD_PAD, MLA_HEADS * 2 * LANES,
              DIFF_HEADS * LANES, DIFF_HEADS * LANES, DIFF_HEADS * 2 * LANES,
              NA_HEADS * NA_DIM, NA_HEADS * NA_DIM, NA_HEADS * NA_DIM)
    kern = functools.partial(
        _post_proj_kernel,
        mla_scale=(MLA_NOPE + MLA_ROPE) ** -0.5 * LOG2E,
        diff_scale=DIFF_QK ** -0.5 * LOG2E,
        na_scale=NA_DIM ** -0.5 * LOG2E)
    return pl.pallas_call(
        kern,
        grid=(T // tm,),
        in_specs=[row(P), row(LANES), row(LANES)] + gain_specs + [
            _resident((1,) + w_uq_p.shape[1:], lambda i: (layer, 0, 0)),
            _resident((1,) + w_ukv_b.shape[1:], lambda i: (layer, 0, 0))],
        out_specs=[row(w) for w in widths],
        out_shape=[jax.ShapeDtypeStruct((T, w), BF16) for w in widths],
        compiler_params=_params("parallel"),
        name="post_proj",
    )(proj, cos_t, sin_t, *gains, w_uq_p, w_ukv_b)


def _flash_kernel(*refs, tk, n_chunks, diff, lam_scale):
    if diff:
        q_ref, k_ref, v_ref, lam_ref, g_ref, o_ref, q_scr, s_scr, acc_scr = refs
    else:
        q_ref, k_ref, v_ref, o_ref, q_scr, s_scr, acc_scr = refs
    q = q_ref[...]
    tq = q.shape[0]
    if diff:
        lane = lax.broadcasted_iota(jnp.int32, q.shape, 1)
        zero = jnp.zeros_like(q)
        q_scr[:tq, :] = jnp.where(lane < DIFF_QK, q, zero)
        q_scr[tq:, :] = jnp.where(lane >= DIFF_QK, q, zero)
    else:
        q_scr[...] = q
    rows = q_scr.shape[0]
    acc_scr[...] = jnp.zeros_like(acc_scr)

    def qk(c, slot):
        start = pl.multiple_of(c * tk, tk)
        s_scr[slot] = lax.dot_general(q_scr[...], k_ref[pl.ds(start, tk), :], NT_DIMS,
                                      preferred_element_type=F32)

    def consume(c, slot, m):
        s = s_scr[slot]
        m_new = jnp.maximum(m, jnp.max(s, axis=-1, keepdims=True))
        alpha = jnp.exp2(m - m_new)
        p = jnp.exp2(s - m_new).astype(BF16)
        start = pl.multiple_of(c * tk, tk)
        acc_scr[...] = alpha * acc_scr[...] + jnp.dot(
            p, v_ref[pl.ds(start, tk), :], preferred_element_type=F32)
        return m_new

    def pair(i, m):
        c = 2 * i
        qk(c + 1, 1)
        m = consume(c, 0, m)
        qk(c + 2, 0)
        return consume(c + 1, 1, m)

    m = jnp.full((rows, 1), -jnp.inf, F32)
    qk(0, 0)
    n_pairs = (n_chunks - 1) // 2
    if n_pairs > 0:
        m = lax.fori_loop(0, n_pairs, pair, m, unroll=True)
    if n_chunks % 2 == 1:
        m = consume(n_chunks - 1, 0, m)
    else:
        qk(n_chunks - 1, 1)
        m = consume(n_chunks - 2, 0, m)
        m = consume(n_chunks - 1, 1, m)
    acc = acc_scr[...]
    o = acc[:, :LANES] / acc[:, LANES:LANES + 1]
    if diff:
        o = o[:tq] - lam_ref[...] * o[tq:]
        ms = jnp.mean(o * o, axis=-1, keepdims=True)
        o = o * lax.rsqrt(ms + EPS) * g_ref[0] * lam_scale
    o_ref[...] = o.astype(o_ref.dtype)


def flash(q, k, v, *, heads, dk, n_q, q_start, n_k, k_start, tq_target, tk_target,
          diff_args=None, name="flash"):
    tq = _pick_tile(math.gcd(n_q, q_start) if q_start else n_q, tq_target)
    tk = _pick_tile(n_k, tk_target, LANES)
    assert k_start % n_k == 0
    q_blk0 = q_start // tq
    k_blk0 = k_start // n_k
    vw = 2 * LANES
    in_specs = [pl.BlockSpec((tq, dk), lambda h, i: (q_blk0 + i, h)),
                pl.BlockSpec((n_k, dk), lambda h, i: (k_blk0, h)),
                pl.BlockSpec((n_k, vw), lambda h, i: (k_blk0, h))]
    args = [q, k, v]
    diff = diff_args is not None
    lam_scale = 1.0
    if diff:
        lam_row, subln, layer, lam_scale = diff_args
        in_specs += [pl.BlockSpec((1, LANES), lambda h, i: (0, 0)),
                     pl.BlockSpec((1, 1, LANES), lambda h, i: (layer, 0, 0))]
        args += [lam_row, subln]
    rows = 2 * tq if diff else tq
    kern = functools.partial(_flash_kernel, tk=tk, n_chunks=n_k // tk, diff=diff,
                             lam_scale=lam_scale)
    return pl.pallas_call(
        kern,
        grid=(heads, n_q // tq),
        in_specs=in_specs,
        out_specs=pl.BlockSpec((tq, LANES), lambda h, i: (i, h)),
        out_shape=jax.ShapeDtypeStruct((n_q, heads * LANES), BF16),
        scratch_shapes=[pltpu.VMEM((rows, dk), BF16),
                        pltpu.VMEM((2, rows, tk), F32),
                        pltpu.VMEM((rows, vw), F32)],
        compiler_params=_params("parallel", "arbitrary"),
        name=name,
    )(*args)


def _with_ones_column(v3):
    T, H, d = v3.shape
    pad = jnp.zeros((T, H, 2 * LANES - d), BF16).at[:, :, 0].set(1.0)
    return jnp.concatenate([v3.astype(BF16), pad], axis=-1).reshape(T, H * 2 * LANES)


def _na_kernel(q_ref, k0_ref, k1_ref, k2_ref, v0_ref, v1_ref, v2_ref, kc_ref, vc_ref, bias_ref,
               o_ref):
    k_refs = (k0_ref, k1_ref, k2_ref, kc_ref)
    v_refs = (v0_ref, v1_ref, v2_ref, vc_ref)
    blk = k0_ref.shape[0]
    for h in range(NA_HEADS):
        cols = slice(h * NA_DIM, (h + 1) * NA_DIM)
        q = q_ref[:, cols]
        s_parts = []
        for b in range(4):
            s = lax.dot_general(q, k_refs[b][:, cols], NT_DIMS, preferred_element_type=F32)
            if b < 3:
                s = s + bias_ref[0, h, :, b * blk:(b + 1) * blk]
            s_parts.append(s)
        m = s_parts[3].max(axis=-1, keepdims=True)
        for b in range(3):
            m = jnp.maximum(m, s_parts[b].max(axis=-1, keepdims=True))
        l = jnp.zeros_like(m)
        acc = jnp.zeros((q.shape[0], NA_DIM), F32)
        for b in range(4):
            p = jnp.exp2(s_parts[b] - m)
            l = l + p.sum(axis=-1, keepdims=True)
            acc = acc + jnp.dot(p.astype(BF16), v_refs[b][:, cols], preferred_element_type=F32)
        o_ref[:, cols] = (acc / l).astype(o_ref.dtype)


def _na_bias_tables(rpb, rows):
    R = NA_ROWS_PER_GROUP
    G = rows // R
    assert rows % R == 0 and G >= 3 and rows >= NA_KH and NA_KH == 2 * R
    tabs = []
    for g in (0, 1, G - 1):
        j = np.arange(R)[:, None, None, None, None]
        qc = np.arange(GRID_W)[None, :, None, None, None]
        b = np.arange(3)[None, None, :, None, None]
        kr = np.arange(R)[None, None, None, :, None]
        kc = np.arange(GRID_W)[None, None, None, None, :]
        r = R * g + j
        rs = np.clip(r - NA_KH // 2, 0, rows - NA_KH)
        keyrow = R * (g - 1 + b) + kr
        wc = np.clip(qc - NA_KW // 2, 0, GRID_W - NA_KW)
        valid = (keyrow >= rs) & (keyrow < rs + NA_KH) & (kc >= wc) & (kc < wc + NA_KW)
        row_rel = np.clip(keyrow - r + NA_KH - 1, 0, 2 * NA_KH - 2)
        col_rel = np.clip(kc - qc + NA_KW - 1, 0, 2 * NA_KW - 2)
        shape = (R, GRID_W, 3, R, GRID_W)
        valid = np.broadcast_to(valid, shape).reshape(R * GRID_W, 3 * R * GRID_W)
        row_rel = np.broadcast_to(row_rel, shape).reshape(R * GRID_W, 3 * R * GRID_W)
        col_rel = np.broadcast_to(col_rel, shape).reshape(R * GRID_W, 3 * R * GRID_W)
        bias = rpb.astype(F32)[:, row_rel, col_rel] * LOG2E
        tabs.append(jnp.where(valid[None], bias, NEG_BIG))
    return jnp.stack(tabs)


def neighborhood(q, k, v, bias_tabs, n_lat):
    blk = NA_ROWS_PER_GROUP * GRID_W
    G = n_lat // blk
    n_ctx = q.shape[0] - n_lat
    assert n_ctx == blk, "context length must equal one key block"
    W = NA_HEADS * NA_DIM

    def kv_spec(off):
        return pl.BlockSpec((blk, W), lambda g: (jnp.clip(g + off, 0, G - 1), 0))

    ctx_spec = pl.BlockSpec((blk, W), lambda g: (G, 0))
    bias_spec = pl.BlockSpec(
        (1, NA_HEADS, blk, 3 * blk),
        lambda g: (jnp.where(g == 0, 0, jnp.where(g == G - 1, 2, 1)), 0, 0, 0))
    return pl.pallas_call(
        _na_kernel,
        grid=(G,),
        in_specs=[pl.BlockSpec((blk, W), lambda g: (g, 0)),
                  kv_spec(-1), kv_spec(0), kv_spec(1),
                  kv_spec(-1), kv_spec(0), kv_spec(1),
                  ctx_spec, ctx_spec, bias_spec],
        out_specs=pl.BlockSpec((blk, W), lambda g: (g, 0)),
        out_shape=jax.ShapeDtypeStruct((n_lat, W), BF16),
        compiler_params=_params("parallel"),
        name="na",
    )(q, k, k, k, v, v, v, k, v, bias_tabs)


def _out_proj_kernel(oa_ref, ob_ref, oc_ref, x_ref, w_ref, g1_ref, gn_ref, sc_ref, sh_ref,
                     rw_ref, rb_ref, x1_o, h2_o, lg_o):
    wa = oa_ref.shape[1]
    wb = ob_ref.shape[1]
    o = jnp.dot(oa_ref[...], w_ref[0, 0:wa, :], preferred_element_type=F32)
    o = o + jnp.dot(ob_ref[...], w_ref[0, wa:wa + wb, :], preferred_element_type=F32)
    o = o + jnp.dot(oc_ref[...], w_ref[0, wa + wb:, :], preferred_element_type=F32)
    x1 = x_ref[...] + g1_ref[0] * o
    x1_o[...] = x1
    ms = jnp.mean(x1 * x1, axis=-1, keepdims=True)
    h2 = x1 * lax.rsqrt(ms + EPS) * gn_ref[0] * (1.0 + sc_ref[0]) + sh_ref[0]
    h2_o[...] = h2.astype(BF16)
    lg_o[...] = jnp.dot(h2, rw_ref[0], preferred_element_type=F32,
                        precision=lax.Precision.HIGHEST) + rb_ref[0]


def out_proj(o_a, o_b, o_c, xa, w_out_b, mods, ffn_g, rw, rb, layer, row_start, stream):
    M = o_a.shape[0]
    D = xa.shape[1]
    tm = ROW_BLOCK
    blk0 = row_start // tm
    row = lambda a: pl.BlockSpec((tm, a.shape[1]), lambda i: (i, 0))
    sel = lambda i: stream
    return pl.pallas_call(
        _out_proj_kernel,
        grid=(M // tm,),
        in_specs=[row(o_a), row(o_b), row(o_c),
                  pl.BlockSpec((tm, D), lambda i: (blk0 + i, 0)),
                  _resident((1,) + w_out_b.shape[1:], lambda i: (layer, 0, 0)),
                  _mod_spec(layer, sel, 2)(D),
                  pl.BlockSpec((1, 1, D), lambda i: (layer, 0, 0)),
                  _mod_spec(layer, sel, 4)(D),
                  _mod_spec(layer, sel, 3)(D),
                  _resident((1, D, LANES), lambda i: (layer, 0, 0)),
                  pl.BlockSpec((1, 1, LANES), lambda i: (layer, 0, 0))],
        out_specs=[pl.BlockSpec((tm, D), lambda i: (i, 0)),
                   pl.BlockSpec((tm, D), lambda i: (i, 0)),
                   pl.BlockSpec((tm, LANES), lambda i: (i, 0))],
        out_shape=[jax.ShapeDtypeStruct((M, D), F32),
                   jax.ShapeDtypeStruct((M, D), BF16),
                   jax.ShapeDtypeStruct((M, LANES), F32)],
        compiler_params=_params("parallel"),
        name="w_out",
    )(o_a, o_b, o_c, xa, w_out_b, mods, ffn_g, mods, mods, rw, rb)


def _deinterleave_matrix():
    p = np.zeros((MXU_DIM, MXU_DIM), np.float32)
    i = np.arange(MXU_DIM // 2)
    p[2 * i, i] = 1.0
    p[2 * i + 1, MXU_DIM // 2 + i] = 1.0
    return jnp.asarray(p, BF16)


def _deint_kernel(w_ref, p_ref, o_ref):
    w = w_ref[0, 0].astype(BF16)
    o_ref[0] = jnp.dot(w, p_ref[...], preferred_element_type=F32).astype(BF16)


def deinterleave_gu(w_gu, layer):
    _, E, D, F2 = w_gu.shape
    return pl.pallas_call(
        _deint_kernel,
        grid=(E, F2 // MXU_DIM),
        in_specs=[pl.BlockSpec((1, 1, D, MXU_DIM), lambda e, j: (layer, e, 0, j)),
                  pl.BlockSpec((MXU_DIM, MXU_DIM), lambda e, j: (0, 0))],
        out_specs=pl.BlockSpec((1, D, MXU_DIM), lambda e, j: (e, 0, j)),
        out_shape=jax.ShapeDtypeStruct((E, D, F2), BF16),
        compiler_params=_params("parallel", "parallel"),
        name="deint_gu",
    )(w_gu, _deinterleave_matrix())


def _moe_kernel(blk_e_ref, n_used_ref, x_ref, wgu_ref, bgu_ref, wd_ref, bd_ref, o_ref, wd_scr):
    i = pl.program_id(0)
    e = blk_e_ref[i]
    e_prev = blk_e_ref[jnp.maximum(i - 1, 0)]

    @pl.when((i == 0) | (e != e_prev))
    def _():
        wd_scr[...] = wd_ref[0, 0].astype(BF16)

    @pl.when(i < n_used_ref[0])
    def _():
        x = x_ref[...]
        gu = jnp.dot(x, wgu_ref[0], preferred_element_type=F32) + bgu_ref[0]
        acts = []
        for c in range(gu.shape[1] // MXU_DIM):
            g = gu[:, c * MXU_DIM:c * MXU_DIM + LANES]
            u = gu[:, c * MXU_DIM + LANES:(c + 1) * MXU_DIM]
            glu = jnp.minimum(g, SWIGLU_LIMIT)
            lin = jnp.clip(u, -SWIGLU_LIMIT, SWIGLU_LIMIT)
            acts.append((glu * jax.nn.sigmoid(SWIGLU_ALPHA * glu) * (lin + 1.0)).astype(BF16))
        act = jnp.concatenate(acts, axis=1)
        o_ref[...] = jnp.dot(act, wd_scr[...], preferred_element_type=F32) + bd_ref[0, 0]

    @pl.when(i >= n_used_ref[0])
    def _():
        o_ref[...] = jnp.zeros_like(o_ref)


def moe_experts(xb, blk_e, n_used, wgu, bgu, w_down, b_down, layer):
    cap, D = xb.shape
    n_blocks = cap // MOE_BLOCK
    F2 = wgu.shape[-1]
    F = w_down.shape[2]
    grid_spec = pltpu.PrefetchScalarGridSpec(
        num_scalar_prefetch=2,
        grid=(n_blocks,),
        in_specs=[pl.BlockSpec((MOE_BLOCK, D), lambda i, e, n: (i, 0)),
                  pl.BlockSpec((1, D, F2), lambda i, e, n: (e[i], 0, 0)),
                  pl.BlockSpec((1, 1, F2), lambda i, e, n: (e[i], 0, 0)),
                  pl.BlockSpec((1, 1, F, D), lambda i, e, n: (layer, e[i], 0, 0)),
                  pl.BlockSpec((1, 1, 1, D), lambda i, e, n: (layer, e[i], 0, 0))],
        out_specs=pl.BlockSpec((MOE_BLOCK, D), lambda i, e, n: (i, 0)),
        scratch_shapes=[pltpu.VMEM((F, D), BF16)],
    )
    return pl.pallas_call(
        _moe_kernel,
        grid_spec=grid_spec,
        out_shape=jax.ShapeDtypeStruct((cap, D), F32),
        compiler_params=_params("arbitrary"),
        name="moe_experts",
    )(blk_e, n_used, xb, wgu, bgu, w_down, b_down)


def _combine_kernel(pc0, pc1, pc2, pc3, pn0, pn1, pn2, pn3, g_ref, x_ref, g2_ref, yb_hbm,
                    o_ref, buf, sem):
    i = pl.program_id(0)
    nb = pl.num_programs(0)
    tb = o_ref.shape[0]
    slot = i % 2

    def issue(pos_refs, s):
        for k in range(TOP_K):
            def body(t, carry, k=k):
                r = pos_refs[k][t]
                pltpu.make_async_copy(yb_hbm.at[pl.ds(r, 1), :],
                                      buf.at[s, pl.ds(k * tb + t, 1), :], sem.at[s]).start()
                return carry
            lax.fori_loop(0, tb, body, 0, unroll=8)

    @pl.when(i == 0)
    def _():
        issue((pc0, pc1, pc2, pc3), 0)

    @pl.when(i + 1 < nb)
    def _():
        issue((pn0, pn1, pn2, pn3), 1 - slot)

    pltpu.make_async_copy(yb_hbm.at[pl.ds(0, TOP_K * tb), :], buf.at[slot], sem.at[slot]).wait()
    y = jnp.zeros(o_ref.shape, F32)
    for k in range(TOP_K):
        y = y + g_ref[:, k:k + 1] * buf[slot, k * tb:(k + 1) * tb, :]
    o_ref[...] = x_ref[...] + g2_ref[0] * y


def moe_combine(yb, pos, gates, x1, mods, layer, n_lat):
    n, D = x1.shape
    tb = COMBINE_BLOCK
    nb = n // tb
    nlb = n_lat // tb
    sel = lambda i: jnp.where(i >= nlb, 1, 0)
    cur = pl.BlockSpec((tb,), lambda i: (i,), memory_space=pltpu.SMEM)
    nxt = pl.BlockSpec((tb,), lambda i: (jnp.minimum(i + 1, nb - 1),), memory_space=pltpu.SMEM)
    return pl.pallas_call(
        _combine_kernel,
        grid=(nb,),
        in_specs=[cur] * TOP_K + [nxt] * TOP_K + [
            pl.BlockSpec((tb, TOP_K), lambda i: (i, 0)),
            pl.BlockSpec((tb, D), lambda i: (i, 0)),
            _mod_spec(layer, sel, 5)(D),
            pl.BlockSpec(memory_space=pl.ANY)],
        out_specs=pl.BlockSpec((tb, D), lambda i: (i, 0)),
        out_shape=jax.ShapeDtypeStruct((n, D), F32),
        scratch_shapes=[pltpu.VMEM((2, TOP_K * tb, D), F32),
                        pltpu.SemaphoreType.DMA((2,))],
        compiler_params=_params("arbitrary"),
        name="moe_combine",
    )(pos[0], pos[1], pos[2], pos[3], pos[0], pos[1], pos[2], pos[3], gates, x1, mods, yb)


def moe_ffn(x1, h2, logits, wgu, bgu, w_down, b_down, mods, layer, n_lat):
    n, D = h2.shape
    top_v, top_e = lax.top_k(logits[:, :N_EXPERTS], TOP_K)
    gates = jax.nn.softmax(top_v, axis=-1)
    n_assign = n * TOP_K
    flat_e = top_e.reshape(-1).astype(jnp.int32)
    onehot = (flat_e[:, None] == jnp.arange(N_EXPERTS, dtype=jnp.int32)[None, :]).astype(jnp.int32)
    csum = jnp.cumsum(onehot, axis=0)
    counts = csum[-1]
    padded = (counts + MOE_BLOCK - 1) // MOE_BLOCK * MOE_BLOCK
    pad_end = jnp.cumsum(padded)
    pad_start = pad_end - padded
    pos = jnp.sum(onehot * (csum - 1 + pad_start[None, :]), axis=1)
    n_blocks = -(-n_assign // MOE_BLOCK) + N_EXPERTS
    cap = n_blocks * MOE_BLOCK
    row_tok = jnp.full((cap,), n, jnp.int32).at[pos].set(
        jnp.arange(n_assign, dtype=jnp.int32) // TOP_K, unique_indices=True)
    blk_e = jnp.minimum(
        jnp.searchsorted(pad_end, jnp.arange(n_blocks, dtype=jnp.int32) * MOE_BLOCK, side='right'),
        N_EXPERTS - 1).astype(jnp.int32)
    n_used = (pad_end[-1:] // MOE_BLOCK).astype(jnp.int32)
    xpad = jnp.concatenate([h2, jnp.zeros((1, D), BF16)], axis=0)
    xb = xpad[row_tok]
    yb = moe_experts(xb, blk_e, n_used, wgu, bgu, w_down, b_down, layer)
    return moe_combine(yb, pos.reshape(n, TOP_K).T, gates, x1, mods, layer, n_lat)


def _rope_tables(n_lat, n_ctx):
    quarter = ROT_DIM // 4
    inv = 1.0 / (ROPE_THETA ** (jnp.arange(quarter, dtype=F32) / quarter))
    t = jnp.arange(n_lat)
    row = (t // GRID_W).astype(F32)
    col = (t % GRID_W).astype(F32)
    ang = jnp.concatenate([row[:, None] * inv, col[:, None] * inv], axis=-1)
    ang = jnp.concatenate([ang, jnp.zeros((n_ctx, ROT_DIM // 2), F32)], axis=0)
    cos, sin = jnp.cos(ang), jnp.sin(ang)
    reps = LANES // ROT_DIM
    cos_t = jnp.tile(jnp.concatenate([cos, cos], axis=-1), (1, reps))
    sin_t = jnp.tile(jnp.concatenate([-sin, sin], axis=-1), (1, reps))
    return cos_t, sin_t


def _pad_w_in(w):
    L, D, _ = w.shape
    a = MLA_Q_RANK + MLA_KV_RANK + MLA_ROPE
    return jnp.concatenate([w[..., :a], jnp.zeros((L, D, _P_DQ - a), w.dtype), w[..., a:]], axis=-1)


def _pad_w_uq(w):
    L, r, _ = w.shape
    w4 = w.reshape(L, r, MLA_HEADS, MLA_NOPE + MLA_ROPE)
    w4 = jnp.concatenate(
        [w4, jnp.zeros((L, r, MLA_HEADS, MLA_HEAD_PAD - MLA_NOPE - MLA_ROPE), w.dtype)], axis=-1)
    return w4.reshape(L, r, MLA_HEADS * MLA_HEAD_PAD)


def _row3(a, width=None, tile=1):
    a = jnp.tile(a.astype(F32), (1, tile))
    if width is not None and width > a.shape[1]:
        a = jnp.concatenate([a, jnp.zeros((a.shape[0], width - a.shape[1]), F32)], axis=1)
    return a[:, None, :]


def kernel(x, c, ctx, c_ctx, w_ada, b_ada, attn_norm, ffn_norm, w_in, mla_qa_norm, w_uq,
           mla_kva_norm, w_ukv, mla_q_gain, mla_knope_gain, mla_kpe_gain, diff_q_gain, diff_k_gain,
           diff_lambda, diff_subln, na_q_gain, na_k_gain, na_rpb, w_out, router_w, router_b,
           w_gu, b_gu, w_down, b_down):
    depth = w_in.shape[0]
    B, N, D = x.shape
    assert B == 1
    C = ctx.shape[1]
    T = N + C
    assert N % ROW_BLOCK == 0 and C % ROW_BLOCK == 0
    rows = N // GRID_W
    cos_t, sin_t = _rope_tables(N, C)
    silu = jnp.stack([jax.nn.silu(c[0]), jax.nn.silu(c_ctx)])
    silu_pad = jnp.concatenate([silu, jnp.zeros((14, D), F32)], axis=0).astype(BF16)
    n_groups = w_gu.shape[-1] // MXU_DIM

    mods = ada_all(silu_pad, w_ada)[:, :2] + b_ada[:, None, :]
    mods = mods.reshape(depth * 2 * N_MOD, 1, D)

    w_in_p = _pad_w_in(w_in).astype(BF16)
    w_uq_p = _pad_w_uq(w_uq).astype(BF16)
    w_ukv_b = w_ukv.astype(BF16)
    w_out_b = w_out.astype(BF16)
    gains = (_row3(mla_qa_norm), _row3(mla_kva_norm), _row3(mla_q_gain, MLA_HEAD_PAD),
             _row3(mla_knope_gain), _row3(mla_kpe_gain, LANES),
             _row3(diff_q_gain, tile=2), _row3(diff_k_gain, tile=2),
             _row3(na_q_gain), _row3(na_k_gain))
    attn_g = _row3(attn_norm)
    ffn_g = _row3(ffn_norm)
    subln = _row3(diff_subln)
    rw = jnp.concatenate([router_w.astype(F32),
                          jnp.zeros((depth, D, LANES - N_EXPERTS), F32)], axis=-1)
    rb = _row3(router_b, LANES)
    bgu_all = b_gu.astype(F32).reshape(depth, N_EXPERTS, n_groups, LANES, 2)
    bgu_all = bgu_all.transpose(0, 1, 2, 4, 3).reshape(depth, N_EXPERTS, 1, n_groups * MXU_DIM)
    bd_all = b_down.astype(F32)[:, :, None, :]

    xa = jnp.concatenate([x[0], ctx[0]], axis=0)
    for l in range(depth):
        with_ctx_out = l < depth - 1
        lam_init = 0.8 - 0.6 * math.exp(-0.3 * l)
        proj = in_proj(xa, attn_g, mods, w_in_p, l, N)
        q_a, k_a, v_a, q_b, k_b, v_b, q_c, k_c, v_c = post_proj(
            proj, cos_t, sin_t, gains, w_uq_p, w_ukv_b, l)

        lf = diff_lambda[l].astype(F32)
        lam = jnp.exp(jnp.sum(lf[0] * lf[1])) - jnp.exp(jnp.sum(lf[2] * lf[3])) + lam_init
        diff_args = (jnp.full((1, LANES), lam, F32), subln, l, 1.0 - lam_init)

        o_a = flash(q_a, k_a, v_a, heads=MLA_HEADS, dk=MLA_HEAD_PAD,
                    n_q=N, q_start=0, n_k=T, k_start=0, tq_target=512, tk_target=1280, name="mla")
        o_b = flash(q_b, k_b, v_b, heads=DIFF_HEADS, dk=2 * DIFF_QK,
                    n_q=N, q_start=0, n_k=T, k_start=0, tq_target=256, tk_target=1280,
                    diff_args=diff_args, name="diff")
        o_c = neighborhood(q_c, k_c, v_c, _na_bias_tables(na_rpb[l], rows), N)
        x1, h2, logits = out_proj(o_a, o_b, o_c, xa, w_out_b, mods, ffn_g, rw, rb, l, 0, 0)

        wgu = deinterleave_gu(w_gu, l)
        if with_ctx_out:
            v_cc = _with_ones_column(v_c[N:].reshape(C, NA_HEADS, NA_DIM))
            oc_a = flash(q_a, k_a, v_a, heads=MLA_HEADS, dk=MLA_HEAD_PAD,
                         n_q=C, q_start=N, n_k=C, k_start=N, tq_target=256, tk_target=256, name="mla_ctx")
            oc_b = flash(q_b, k_b, v_b, heads=DIFF_HEADS, dk=2 * DIFF_QK,
                         n_q=C, q_start=N, n_k=C, k_start=N, tq_target=256, tk_target=256,
                         diff_args=diff_args, name="diff_ctx")
            oc_c = flash(q_c, k_c[N:], v_cc, heads=NA_HEADS, dk=NA_DIM,
                         n_q=C, q_start=N, n_k=C, k_start=0, tq_target=256, tk_target=256, name="na_ctx")
            x1c, h2c, logits_c = out_proj(oc_a, oc_b, oc_c, xa, w_out_b, mods, ffn_g, rw, rb, l, N, 1)
            x1 = jnp.concatenate([x1, x1c], axis=0)
            h2 = jnp.concatenate([h2, h2c], axis=0)
            logits = jnp.concatenate([logits, logits_c], axis=0)
        xa = moe_ffn(x1, h2, logits, wgu, bgu_all[l], w_down, bd_all, mods, l, N)
    return xa[:N][None]
```

```python
import functools
import math

import numpy as np
import jax
import jax.numpy as jnp
from jax import lax
from jax.experimental import pallas as pl
from jax.experimental.pallas import tpu as pltpu

F32 = jnp.float32
BF16 = jnp.bfloat16

GRID_W = 64
EPS = 1e-6
ROPE_THETA = 10000.0
ROT_DIM = 64
MLA_HEADS = 8
MLA_NOPE = 128
MLA_ROPE = 64
MLA_V = 128
MLA_Q_RANK = 512
MLA_KV_RANK = 256
MLA_HEAD_PAD = 256
DIFF_HEADS = 4
DIFF_QK = 64
DIFF_V = 128
NA_HEADS = 4
NA_DIM = 128
NA_KH = 8
NA_KW = 16
NA_ROWS_PER_GROUP = 4
N_EXPERTS = 32
TOP_K = 4
D_EXPERT = 768
SWIGLU_ALPHA = 1.702
SWIGLU_LIMIT = 7.0
MOE_BLOCK = 256
COMBINE_BLOCK = 128
ROW_BLOCK = 256
N_MOD = 6
NEG_BIG = -1e30
LOG2E = math.log2(math.e)

LANES = 128
MXU_DIM = 256
VMEM_LIMIT = 56 * 1024 * 1024

NT_DIMS = (((1,), (1,)), ((), ()))

_P_QA, _P_KVA, _P_KPE, _P_DQ, _P_DK, _P_DV, _P_NQ, _P_NK, _P_NV, _P_END = (
    0, 512, 768, 1024, 1536, 2048, 2560, 3072, 3584, 4096)


def _pick_tile(m, target, mult=8):
    best = None
    for t in range(mult, min(m, target) + 1, mult):
        if m % t == 0:
            best = t
    assert best is not None, (m, target)
    return best


def _params(*sem):
    return pltpu.CompilerParams(dimension_semantics=sem, vmem_limit_bytes=VMEM_LIMIT)


def _resident(block_shape, index_map):
    return pl.BlockSpec(block_shape, index_map, pipeline_mode=pl.Buffered(1))


def _mod_spec(layer, sel_fn, k):
    return lambda D: pl.BlockSpec((1, 1, D), lambda i: ((layer * 2 + sel_fn(i)) * N_MOD + k, 0, 0))


def _ada_kernel(a_ref, w_ref, o_ref):
    o_ref[0] = jnp.dot(a_ref[...], w_ref[0].astype(BF16), preferred_element_type=F32)


def ada_all(silu_pad, w_ada):
    L, D, N6 = w_ada.shape
    tn = _pick_tile(N6, 512, LANES)
    return pl.pallas_call(
        _ada_kernel,
        grid=(L, N6 // tn),
        in_specs=[pl.BlockSpec((16, D), lambda l, j: (0, 0)),
                  pl.BlockSpec((1, D, tn), lambda l, j: (l, 0, j))],
        out_specs=pl.BlockSpec((1, 16, tn), lambda l, j: (l, 0, j)),
        out_shape=jax.ShapeDtypeStruct((L, 16, N6), F32),
        compiler_params=_params("parallel", "parallel"),
        name="ada",
    )(silu_pad, w_ada)


def _norm_mm_kernel(x_ref, g_ref, sc_ref, sh_ref, w_ref, o_ref):
    x = x_ref[...]
    ms = jnp.mean(x * x, axis=-1, keepdims=True)
    h = x * lax.rsqrt(ms + EPS) * g_ref[0] * (1.0 + sc_ref[0]) + sh_ref[0]
    o_ref[...] = jnp.dot(h.astype(BF16), w_ref[0], preferred_element_type=F32)


def in_proj(xa, norm_g, mods, w_in_p, layer, n_lat):
    T, D = xa.shape
    P = w_in_p.shape[-1]
    nlb = n_lat // ROW_BLOCK
    sel = lambda i: jnp.where(i >= nlb, 1, 0)
    return pl.pallas_call(
        _norm_mm_kernel,
        grid=(T // ROW_BLOCK,),
        in_specs=[pl.BlockSpec((ROW_BLOCK, D), lambda i: (i, 0)),
                  pl.BlockSpec((1, 1, D), lambda i: (layer, 0, 0)),
                  _mod_spec(layer, sel, 1)(D),
                  _mod_spec(layer, sel, 0)(D),
                  _resident((1, D, P), lambda i: (layer, 0, 0))],
        out_specs=pl.BlockSpec((ROW_BLOCK, P), lambda i: (i, 0)),
        out_shape=jax.ShapeDtypeStruct((T, P), F32),
        compiler_params=_params("parallel"),
        name="w_in",
    )(xa, norm_g, mods, mods, w_in_p)


def _post_proj_kernel(p_ref, cos_ref, sin_ref, gqa_ref, gkva_ref, gq_ref, gkn_ref, gkpe_ref,
                      gdq_ref, gdk_ref, gnq_ref, gnk_ref, wuq_ref, wukv_ref,
                      qa_o, ka_o, va_o, qb_o, kb_o, vb_o, qc_o, kc_o, vc_o,
                      *, mla_scale, diff_scale, na_scale):
    tm = p_ref.shape[0]
    cos_t = cos_ref[...]
    sin_t = sin_ref[...]
    lane = lax.broadcasted_iota(jnp.int32, (tm, LANES), 1)
    lo = lane < ROT_DIM
    first_half = (lane & (ROT_DIM - 1)) < ROT_DIM // 2
    ones_col = jnp.where(lane == 0, 1.0, 0.0).astype(BF16)

    def rope(x):
        sw = jnp.where(first_half, pltpu.roll(x, LANES - ROT_DIM // 2, 1),
                       pltpu.roll(x, ROT_DIM // 2, 1))
        return x * cos_t + sw * sin_t

    def rinv(x, n):
        return lax.rsqrt(jnp.sum(x * x, axis=-1, keepdims=True) * (1.0 / n) + EPS)

    qa = p_ref[:, _P_QA:_P_KVA]
    qa_n = (qa * rinv(qa, MLA_Q_RANK) * gqa_ref[0]).astype(BF16)
    qf = jnp.dot(qa_n, wuq_ref[0], preferred_element_type=F32)
    g_q = gq_ref[0]
    for h in range(MLA_HEADS):
        c0 = h * MLA_HEAD_PAD
        blk = qf[:, c0:c0 + MLA_HEAD_PAD]
        r = rinv(blk, MLA_NOPE + MLA_ROPE) * mla_scale
        qa_o[:, c0:c0 + LANES] = (blk[:, :LANES] * r * g_q[:, :LANES]).astype(BF16)
        qa_o[:, c0 + LANES:c0 + 2 * LANES] = rope(blk[:, LANES:] * r * g_q[:, LANES:]).astype(BF16)

    kva = p_ref[:, _P_KVA:_P_KPE]
    kva_n = (kva * rinv(kva, MLA_KV_RANK) * gkva_ref[0]).astype(BF16)
    kv = jnp.dot(kva_n, wukv_ref[0], preferred_element_type=F32)
    kpe = p_ref[:, _P_KPE:_P_KPE + LANES]
    kpe_r = rope(kpe * rinv(kpe, MLA_ROPE) * gkpe_ref[0]).astype(BF16)
    for h in range(MLA_HEADS):
        c0 = h * MLA_HEAD_PAD
        kn = kv[:, c0:c0 + LANES]
        ka_o[:, c0:c0 + LANES] = (kn * rinv(kn, MLA_NOPE) * gkn_ref[0]).astype(BF16)
        ka_o[:, c0 + LANES:c0 + 2 * LANES] = kpe_r
        va_o[:, c0:c0 + LANES] = kv[:, c0 + LANES:c0 + 2 * LANES].astype(BF16)
        va_o[:, c0 + LANES:c0 + 2 * LANES] = ones_col

    def subhead_norm_rope(x, g_ref, scale):
        sq = x * x
        s_lo = jnp.sum(jnp.where(lo, sq, 0.0), axis=-1, keepdims=True)
        s_hi = jnp.sum(jnp.where(lo, 0.0, sq), axis=-1, keepdims=True)
        r = jnp.where(lo, lax.rsqrt(s_lo * (1.0 / DIFF_QK) + EPS),
                      lax.rsqrt(s_hi * (1.0 / DIFF_QK) + EPS))
        return (rope(x * r * g_ref[0]) * scale).astype(BF16)

    for h in range(DIFF_HEADS):
        c0 = h * LANES
        qb_o[:, c0:c0 + LANES] = subhead_norm_rope(p_ref[:, _P_DQ + c0:_P_DQ + c0 + LANES], gdq_ref, diff_scale)
        kb_o[:, c0:c0 + LANES] = subhead_norm_rope(p_ref[:, _P_DK + c0:_P_DK + c0 + LANES], gdk_ref, 1.0)
        vb_o[:, 2 * c0:2 * c0 + LANES] = p_ref[:, _P_DV + c0:_P_DV + c0 + LANES].astype(BF16)
        vb_o[:, 2 * c0 + LANES:2 * c0 + 2 * LANES] = ones_col

    for h in range(NA_HEADS):
        c0 = h * NA_DIM
        x = p_ref[:, _P_NQ + c0:_P_NQ + c0 + NA_DIM]
        qc_o[:, c0:c0 + NA_DIM] = (x * rinv(x, NA_DIM) * gnq_ref[0] * na_scale).astype(BF16)
        x = p_ref[:, _P_NK + c0:_P_NK + c0 + NA_DIM]
        kc_o[:, c0:c0 + NA_DIM] = (x * rinv(x, NA_DIM) * gnk_ref[0]).astype(BF16)
    vc_o[...] = p_ref[:, _P_NV:_P_END].astype(BF16)


def post_proj(proj, cos_t, sin_t, gains, w_uq_p, w_ukv_b, layer):
    T, P = proj.shape
    tm = ROW_BLOCK
    row = lambda w: pl.BlockSpec((tm, w), lambda i: (i, 0))
    gain_specs = [pl.BlockSpec((1, 1, g.shape[-1]), lambda i: (layer, 0, 0)) for g in gains]
    widths = (MLA_HEADS * MLA_HEAD_PAD, MLA_HEADS * MLA_HEAD_PAD, MLA_HEADS * 2 * LANES,
              DIFF_HEADS * LANES, DIFF_HEADS * LANES, DIFF_HEADS * 2 * LANES,
              NA_HEADS * NA_DIM, NA_HEADS * NA_DIM, NA_HEADS * NA_DIM)
    kern = functools.partial(
        _post_proj_kernel,
        mla_scale=(MLA_NOPE + MLA_ROPE) ** -0.5 * LOG2E,
        diff_scale=DIFF_QK ** -0.5 * LOG2E,
        na_scale=NA_DIM ** -0.5 * LOG2E)
    return pl.pallas_call(
        kern,
        grid=(T // tm,),
        in_specs=[row(P), row(LANES), row(LANES)] + gain_specs + [
            _resident((1,) + w_uq_p.shape[1:], lambda i: (layer, 0, 0)),
            _resident((1,) + w_ukv_b.shape[1:], lambda i: (layer, 0, 0))],
        out_specs=[row(w) for w in widths],
        out_shape=[jax.ShapeDtypeStruct((T, w), BF16) for w in widths],
        compiler_params=_params("parallel"),
        name="post_proj",
    )(proj, cos_t, sin_t, *gains, w_uq_p, w_ukv_b)


def _flash_kernel(*refs, tk, n_chunks, diff, lam_scale):
    if diff:
        q_ref, k_ref, v_ref, lam_ref, g_ref, o_ref, q_scr, s_scr, acc_scr = refs
    else:
        q_ref, k_ref, v_ref, o_ref, q_scr, s_scr, acc_scr = refs
    q = q_ref[...]
    tq = q.shape[0]
    if diff:
        lane = lax.broadcasted_iota(jnp.int32, q.shape, 1)
        zero = jnp.zeros_like(q)
        q_scr[:tq, :] = jnp.where(lane < DIFF_QK, q, zero)
        q_scr[tq:, :] = jnp.where(lane >= DIFF_QK, q, zero)
    else:
        q_scr[...] = q
    rows = q_scr.shape[0]
    acc_scr[...] = jnp.zeros_like(acc_scr)

    def qk(c, slot):
        start = pl.multiple_of(c * tk, tk)
        s_scr[slot] = lax.dot_general(q_scr[...], k_ref[pl.ds(start, tk), :], NT_DIMS,
                                      preferred_element_type=F32)

    def consume(c, slot, m):
        s = s_scr[slot]
        m_new = jnp.maximum(m, jnp.max(s, axis=-1, keepdims=True))
        alpha = jnp.exp2(m - m_new)
        p = jnp.exp2(s - m_new).astype(BF16)
        start = pl.multiple_of(c * tk, tk)
        acc_scr[...] = alpha * acc_scr[...] + jnp.dot(
            p, v_ref[pl.ds(start, tk), :], preferred_element_type=F32)
        return m_new

    def pair(i, m):
        c = 2 * i
        qk(c + 1, 1)
        m = consume(c, 0, m)
        qk(c + 2, 0)
        return consume(c + 1, 1, m)

    m = jnp.full((rows, 1), -jnp.inf, F32)
    qk(0, 0)
    n_pairs = (n_chunks - 1) // 2
    if n_pairs > 0:
        m = lax.fori_loop(0, n_pairs, pair, m, unroll=True)
    if n_chunks % 2 == 1:
        m = consume(n_chunks - 1, 0, m)
    else:
        qk(n_chunks - 1, 1)
        m = consume(n_chunks - 2, 0, m)
        m = consume(n_chunks - 1, 1, m)
    acc = acc_scr[...]
    o = acc[:, :LANES] / acc[:, LANES:LANES + 1]
    if diff:
        o = o[:tq] - lam_ref[...] * o[tq:]
        ms = jnp.mean(o * o, axis=-1, keepdims=True)
        o = o * lax.rsqrt(ms + EPS) * g_ref[0] * lam_scale
    o_ref[...] = o.astype(o_ref.dtype)


def flash(q, k, v, *, heads, dk, n_q, q_start, n_k, k_start, tq_target, tk_target,
          diff_args=None, name="flash"):
    tq = _pick_tile(math.gcd(n_q, q_start) if q_start else n_q, tq_target)
    tk = _pick_tile(n_k, tk_target, LANES)
    assert k_start % n_k == 0
    q_blk0 = q_start // tq
    k_blk0 = k_start // n_k
    vw = 2 * LANES
    in_specs = [pl.BlockSpec((tq, dk), lambda h, i: (q_blk0 + i, h)),
                pl.BlockSpec((n_k, dk), lambda h, i: (k_blk0, h)),
                pl.BlockSpec((n_k, vw), lambda h, i: (k_blk0, h))]
    args = [q, k, v]
    diff = diff_args is not None
    lam_scale = 1.0
    if diff:
        lam_row, subln, layer, lam_scale = diff_args
        in_specs += [pl.BlockSpec((1, LANES), lambda h, i: (0, 0)),
                     pl.BlockSpec((1, 1, LANES), lambda h, i: (layer, 0, 0))]
        args += [lam_row, subln]
    rows = 2 * tq if diff else tq
    kern = functools.partial(_flash_kernel, tk=tk, n_chunks=n_k // tk, diff=diff,
                             lam_scale=lam_scale)
    return pl.pallas_call(
        kern,
        grid=(heads, n_q // tq),
        in_specs=in_specs,
        out_specs=pl.BlockSpec((tq, LANES), lambda h, i: (i, h)),
        out_shape=jax.ShapeDtypeStruct((n_q, heads * LANES), BF16),
        scratch_shapes=[pltpu.VMEM((rows, dk), BF16),
                        pltpu.VMEM((2, rows, tk), F32),
                        pltpu.VMEM((rows, vw), F32)],
        compiler_params=_params("parallel", "arbitrary"),
        name=name,
    )(*args)


def _with_ones_column(v3):
    T, H, d = v3.shape
    pad = jnp.zeros((T, H, 2 * LANES - d), BF16).at[:, :, 0].set(1.0)
    return jnp.concatenate([v3.astype(BF16), pad], axis=-1).reshape(T, H * 2 * LANES)


def _na_kernel(q_ref, k0_ref, k1_ref, k2_ref, v0_ref, v1_ref, v2_ref, kc_ref, vc_ref, bias_ref,
               o_ref):
    k_refs = (k0_ref, k1_ref, k2_ref, kc_ref)
    v_refs = (v0_ref, v1_ref, v2_ref, vc_ref)
    blk = k0_ref.shape[0]
    for h in range(NA_HEADS):
        cols = slice(h * NA_DIM, (h + 1) * NA_DIM)
        q = q_ref[:, cols]
        s_parts = []
        for b in range(4):
            s = lax.dot_general(q, k_refs[b][:, cols], NT_DIMS, preferred_element_type=F32)
            if b < 3:
                s = s + bias_ref[0, 0, h, :, b * blk:(b + 1) * blk]
            s_parts.append(s)
        m = s_parts[3].max(axis=-1, keepdims=True)
        for b in range(3):
            m = jnp.maximum(m, s_parts[b].max(axis=-1, keepdims=True))
        l = jnp.zeros_like(m)
        acc = jnp.zeros((q.shape[0], NA_DIM), F32)
        for b in range(4):
            p = jnp.exp2(s_parts[b] - m)
            l = l + p.sum(axis=-1, keepdims=True)
            acc = acc + jnp.dot(p.astype(BF16), v_refs[b][:, cols], preferred_element_type=F32)
        o_ref[:, cols] = (acc / l).astype(o_ref.dtype)


def _na_bias_tables(rpb, rows):
    R = NA_ROWS_PER_GROUP
    G = rows // R
    assert rows % R == 0 and G >= 3 and rows >= NA_KH and NA_KH == 2 * R
    n_rr, n_cr = 2 * NA_KH - 1, 2 * NA_KW - 1
    qc = np.arange(GRID_W)[:, None]
    kc = np.arange(GRID_W)[None, :]
    wc = np.clip(qc - NA_KW // 2, 0, GRID_W - NA_KW)
    valid_c = (kc >= wc) & (kc < wc + NA_KW)
    col_rel = np.clip(kc - qc + NA_KW - 1, 0, n_cr - 1).reshape(-1)
    c_sel = (np.arange(n_cr)[:, None] == col_rel[None, :]).astype(np.float32)
    tabs = []
    for g in (0, 1, G - 1):
        j = np.arange(R)[:, None, None]
        b = np.arange(3)[None, :, None]
        kr = np.arange(R)[None, None, :]
        r = R * g + j
        rs = np.clip(r - NA_KH // 2, 0, rows - NA_KH)
        keyrow = R * (g - 1 + b) + kr
        valid_r = np.broadcast_to((keyrow >= rs) & (keyrow < rs + NA_KH), (R, 3, R))
        row_rel = np.broadcast_to(np.clip(keyrow - r + NA_KH - 1, 0, n_rr - 1), (R, 3, R)).reshape(-1)
        r_sel = (row_rel[:, None] == np.arange(n_rr)[None, :]).astype(np.float32)
        t = jnp.einsum('ar,lhrc,cq->lhaq', r_sel, rpb.astype(F32), c_sel,
                       precision=lax.Precision.HIGHEST)
        L, H = t.shape[:2]
        t = t.reshape(L, H, R, 3, R, GRID_W, GRID_W).transpose(0, 1, 2, 5, 3, 4, 6)
        valid = valid_r[:, None, :, :, None] & valid_c[None, :, None, None, :]
        t = jnp.where(valid[None, None], t * LOG2E, NEG_BIG)
        tabs.append(t.reshape(L, H, R * GRID_W, 3 * R * GRID_W))
    return jnp.stack(tabs, axis=1)


def neighborhood(q, k, v, bias_tabs, n_lat, layer):
    blk = NA_ROWS_PER_GROUP * GRID_W
    G = n_lat // blk
    n_ctx = q.shape[0] - n_lat
    assert n_ctx == blk, "context length must equal one key block"
    W = NA_HEADS * NA_DIM

    def kv_spec(off):
        return pl.BlockSpec((blk, W), lambda g: (jnp.clip(g + off, 0, G - 1), 0))

    ctx_spec = pl.BlockSpec((blk, W), lambda g: (G, 0))
    bias_spec = pl.BlockSpec(
        (1, 1, NA_HEADS, blk, 3 * blk),
        lambda g: (layer, jnp.where(g == 0, 0, jnp.where(g == G - 1, 2, 1)), 0, 0, 0))
    return pl.pallas_call(
        _na_kernel,
        grid=(G,),
        in_specs=[pl.BlockSpec((blk, W), lambda g: (g, 0)),
                  kv_spec(-1), kv_spec(0), kv_spec(1),
                  kv_spec(-1), kv_spec(0), kv_spec(1),
                  ctx_spec, ctx_spec, bias_spec],
        out_specs=pl.BlockSpec((blk, W), lambda g: (g, 0)),
        out_shape=jax.ShapeDtypeStruct((n_lat, W), BF16),
        compiler_params=_params("parallel"),
        name="na",
    )(q, k, k, k, v, v, v, k, v, bias_tabs)


def _out_proj_kernel(oa_ref, ob_ref, oc_ref, x_ref, w_ref, g1_ref, gn_ref, sc_ref, sh_ref,
                     rw_ref, rb_ref, x1_o, h2_o, lg_o):
    wa = oa_ref.shape[1]
    wb = ob_ref.shape[1]
    o = jnp.dot(oa_ref[...], w_ref[0, 0:wa, :], preferred_element_type=F32)
    o = o + jnp.dot(ob_ref[...], w_ref[0, wa:wa + wb, :], preferred_element_type=F32)
    o = o + jnp.dot(oc_ref[...], w_ref[0, wa + wb:, :], preferred_element_type=F32)
    x1 = x_ref[...] + g1_ref[0] * o
    x1_o[...] = x1
    ms = jnp.mean(x1 * x1, axis=-1, keepdims=True)
    h2 = x1 * lax.rsqrt(ms + EPS) * gn_ref[0] * (1.0 + sc_ref[0]) + sh_ref[0]
    h2_hi = h2.astype(BF16)
    h2_o[...] = h2_hi
    h2_lo = (h2 - h2_hi.astype(F32)).astype(BF16)
    a = (jnp.dot(h2_hi, rw_ref[0], preferred_element_type=F32)
         + jnp.dot(h2_lo, rw_ref[0], preferred_element_type=F32))
    lg_o[...] = a[:, :LANES] + a[:, LANES:] + rb_ref[0]


def out_proj(o_a, o_b, o_c, xa, w_out_b, mods, ffn_g, rw, rb, layer, row_start, stream):
    M = o_a.shape[0]
    D = xa.shape[1]
    tm = ROW_BLOCK
    blk0 = row_start // tm
    row = lambda a: pl.BlockSpec((tm, a.shape[1]), lambda i: (i, 0))
    sel = lambda i: stream
    return pl.pallas_call(
        _out_proj_kernel,
        grid=(M // tm,),
        in_specs=[row(o_a), row(o_b), row(o_c),
                  pl.BlockSpec((tm, D), lambda i: (blk0 + i, 0)),
                  _resident((1,) + w_out_b.shape[1:], lambda i: (layer, 0, 0)),
                  _mod_spec(layer, sel, 2)(D),
                  pl.BlockSpec((1, 1, D), lambda i: (layer, 0, 0)),
                  _mod_spec(layer, sel, 4)(D),
                  _mod_spec(layer, sel, 3)(D),
                  _resident((1, D, 2 * LANES), lambda i: (layer, 0, 0)),
                  pl.BlockSpec((1, 1, LANES), lambda i: (layer, 0, 0))],
        out_specs=[pl.BlockSpec((tm, D), lambda i: (i, 0)),
                   pl.BlockSpec((tm, D), lambda i: (i, 0)),
                   pl.BlockSpec((tm, LANES), lambda i: (i, 0))],
        out_shape=[jax.ShapeDtypeStruct((M, D), F32),
                   jax.ShapeDtypeStruct((M, D), BF16),
                   jax.ShapeDtypeStruct((M, LANES), F32)],
        compiler_params=_params("parallel"),
        name="w_out",
    )(o_a, o_b, o_c, xa, w_out_b, mods, ffn_g, mods, mods, rw, rb)


def _deinterleave_matrix():
    p = np.zeros((MXU_DIM, MXU_DIM), np.float32)
    i = np.arange(MXU_DIM // 2)
    p[2 * i, i] = 1.0
    p[2 * i + 1, MXU_DIM // 2 + i] = 1.0
    return jnp.asarray(p, BF16)


def _deint_kernel(w_ref, p_ref, o_ref):
    for c in range(w_ref.shape[-1] // MXU_DIM):
        cols = slice(c * MXU_DIM, (c + 1) * MXU_DIM)
        w = w_ref[0, 0, :, cols].astype(BF16)
        o_ref[0, :, cols] = jnp.dot(w, p_ref[...], preferred_element_type=F32).astype(BF16)


def deinterleave_gu(w_gu, layer):
    _, E, D, F2 = w_gu.shape
    tr = _pick_tile(D, ROW_BLOCK)
    return pl.pallas_call(
        _deint_kernel,
        grid=(E, D // tr),
        in_specs=[pl.BlockSpec((1, 1, tr, F2), lambda e, j: (layer, e, j, 0)),
                  pl.BlockSpec((MXU_DIM, MXU_DIM), lambda e, j: (0, 0))],
        out_specs=pl.BlockSpec((1, tr, F2), lambda e, j: (e, j, 0)),
        out_shape=jax.ShapeDtypeStruct((E, D, F2), BF16),
        compiler_params=_params("parallel", "parallel"),
        name="deint_gu",
    )(w_gu, _deinterleave_matrix())


def _moe_kernel(blk_e_ref, n_used_ref, x_ref, wgu_ref, bgu_ref, wd_ref, bd_ref, o_ref, wd_scr):
    i = pl.program_id(0)
    e = blk_e_ref[i]
    e_prev = blk_e_ref[jnp.maximum(i - 1, 0)]

    @pl.when((i == 0) | (e != e_prev))
    def _():
        wd_scr[...] = wd_ref[0, 0].astype(BF16)

    @pl.when(i < n_used_ref[0])
    def _():
        x = x_ref[...]
        gu = jnp.dot(x, wgu_ref[0], preferred_element_type=F32) + bgu_ref[0]
        acts = []
        for c in range(gu.shape[1] // MXU_DIM):
            g = gu[:, c * MXU_DIM:c * MXU_DIM + LANES]
            u = gu[:, c * MXU_DIM + LANES:(c + 1) * MXU_DIM]
            glu = jnp.minimum(g, SWIGLU_LIMIT)
            lin = jnp.clip(u, -SWIGLU_LIMIT, SWIGLU_LIMIT)
            acts.append((glu * jax.nn.sigmoid(SWIGLU_ALPHA * glu) * (lin + 1.0)).astype(BF16))
        act = jnp.concatenate(acts, axis=1)
        o_ref[...] = jnp.dot(act, wd_scr[...], preferred_element_type=F32) + bd_ref[0, 0]

    @pl.when(i >= n_used_ref[0])
    def _():
        o_ref[...] = jnp.zeros_like(o_ref)


def moe_experts(xb, blk_e, n_used, wgu, bgu, w_down, b_down, layer):
    cap, D = xb.shape
    n_blocks = cap // MOE_BLOCK
    F2 = wgu.shape[-1]
    F = w_down.shape[2]
    grid_spec = pltpu.PrefetchScalarGridSpec(
        num_scalar_prefetch=2,
        grid=(n_blocks,),
        in_specs=[pl.BlockSpec((MOE_BLOCK, D), lambda i, e, n: (i, 0)),
                  pl.BlockSpec((1, D, F2), lambda i, e, n: (e[i], 0, 0)),
                  pl.BlockSpec((1, 1, F2), lambda i, e, n: (e[i], 0, 0)),
                  pl.BlockSpec((1, 1, F, D), lambda i, e, n: (layer, e[i], 0, 0)),
                  pl.BlockSpec((1, 1, 1, D), lambda i, e, n: (layer, e[i], 0, 0))],
        out_specs=pl.BlockSpec((MOE_BLOCK, D), lambda i, e, n: (i, 0)),
        scratch_shapes=[pltpu.VMEM((F, D), BF16)],
    )
    return pl.pallas_call(
        _moe_kernel,
        grid_spec=grid_spec,
        out_shape=jax.ShapeDtypeStruct((cap, D), F32),
        compiler_params=_params("arbitrary"),
        name="moe_experts",
    )(blk_e, n_used, xb, wgu, bgu, w_down, b_down)


def _combine_kernel(pc0, pc1, pc2, pc3, pn0, pn1, pn2, pn3, g_ref, x_ref, g2_ref, yb_hbm,
                    o_ref, buf, sem):
    i = pl.program_id(0)
    nb = pl.num_programs(0)
    tb = o_ref.shape[0]
    slot = i % 2

    def issue(pos_refs, s):
        for k in range(TOP_K):
            def body(t, carry, k=k):
                r = pos_refs[k][t]
                pltpu.make_async_copy(yb_hbm.at[pl.ds(r, 1), :],
                                      buf.at[s, pl.ds(k * tb + t, 1), :], sem.at[s]).start()
                return carry
            lax.fori_loop(0, tb, body, 0, unroll=8)

    @pl.when(i == 0)
    def _():
        issue((pc0, pc1, pc2, pc3), 0)

    @pl.when(i + 1 < nb)
    def _():
        issue((pn0, pn1, pn2, pn3), 1 - slot)

    pltpu.make_async_copy(yb_hbm.at[pl.ds(0, TOP_K * tb), :], buf.at[slot], sem.at[slot]).wait()
    y = jnp.zeros(o_ref.shape, F32)
    for k in range(TOP_K):
        y = y + g_ref[:, k:k + 1] * buf[slot, k * tb:(k + 1) * tb, :]
    o_ref[...] = x_ref[...] + g2_ref[0] * y


def moe_combine(yb, pos, gates, x1, mods, layer, n_lat):
    n, D = x1.shape
    tb = COMBINE_BLOCK
    nb = n // tb
    nlb = n_lat // tb
    sel = lambda i: jnp.where(i >= nlb, 1, 0)
    cur = pl.BlockSpec((tb,), lambda i: (i,), memory_space=pltpu.SMEM)
    nxt = pl.BlockSpec((tb,), lambda i: (jnp.minimum(i + 1, nb - 1),), memory_space=pltpu.SMEM)
    return pl.pallas_call(
        _combine_kernel,
        grid=(nb,),
        in_specs=[cur] * TOP_K + [nxt] * TOP_K + [
            pl.BlockSpec((tb, TOP_K), lambda i: (i, 0)),
            pl.BlockSpec((tb, D), lambda i: (i, 0)),
            _mod_spec(layer, sel, 5)(D),
            pl.BlockSpec(memory_space=pl.ANY)],
        out_specs=pl.BlockSpec((tb, D), lambda i: (i, 0)),
        out_shape=jax.ShapeDtypeStruct((n, D), F32),
        scratch_shapes=[pltpu.VMEM((2, TOP_K * tb, D), F32),
                        pltpu.SemaphoreType.DMA((2,))],
        compiler_params=_params("arbitrary"),
        name="moe_combine",
    )(pos[0], pos[1], pos[2], pos[3], pos[0], pos[1], pos[2], pos[3], gates, x1, mods, yb)


def _route_kernel(lg_ref, e_o, g_o, r_o, cnt_o, run_scr):
    i = pl.program_id(0)

    @pl.when(i == 0)
    def _():
        run_scr[...] = jnp.zeros_like(run_scr)

    tb = lg_ref.shape[0]
    lane = lax.broadcasted_iota(jnp.int32, (tb, LANES), 1)
    x = jnp.where(lane < N_EXPERTS, lg_ref[...], -jnp.inf)
    vals, hots = [], []
    for k in range(TOP_K):
        m = jnp.max(x, axis=-1, keepdims=True)
        idx = jnp.min(jnp.where(x == m, lane, LANES), axis=-1, keepdims=True)
        hot = lane == idx
        e_o[:, k:k + 1] = idx
        vals.append(m)
        hots.append(hot)
        x = jnp.where(hot, -jnp.inf, x)
    exps = [jnp.exp(v - vals[0]) for v in vals]
    denom = exps[0] + exps[1] + exps[2] + exps[3]
    for k in range(TOP_K):
        g_o[:, k:k + 1] = exps[k] / denom
    picked = (hots[0] | hots[1] | hots[2] | hots[3]).astype(BF16)
    r_idx = lax.broadcasted_iota(jnp.int32, (tb, tb), 0)
    c_idx = lax.broadcasted_iota(jnp.int32, (tb, tb), 1)
    before = (c_idx < r_idx).astype(BF16)
    seen = jnp.dot(before, picked, preferred_element_type=F32) + run_scr[...]
    for k in range(TOP_K):
        rank = jnp.sum(jnp.where(hots[k], seen, 0.0), axis=-1, keepdims=True)
        r_o[:, k:k + 1] = rank.astype(jnp.int32)
    run_scr[...] = run_scr[...] + jnp.sum(picked.astype(F32), axis=0, keepdims=True)
    cnt_o[...] = run_scr[...]


def route(logits):
    n = logits.shape[0]
    tb = ROW_BLOCK
    out = lambda: pl.BlockSpec((tb, TOP_K), lambda i: (i, 0))
    return pl.pallas_call(
        _route_kernel,
        grid=(n // tb,),
        in_specs=[pl.BlockSpec((tb, LANES), lambda i: (i, 0))],
        out_specs=[out(), out(), out(), pl.BlockSpec((1, LANES), lambda i: (0, 0))],
        out_shape=[jax.ShapeDtypeStruct((n, TOP_K), jnp.int32),
                   jax.ShapeDtypeStruct((n, TOP_K), F32),
                   jax.ShapeDtypeStruct((n, TOP_K), jnp.int32),
                   jax.ShapeDtypeStruct((1, LANES), F32)],
        scratch_shapes=[pltpu.VMEM((1, LANES), F32)],
        compiler_params=_params("arbitrary"),
        name="route",
    )(logits)


def moe_ffn(x1, h2, logits, wgu, bgu, w_down, b_down, mods, layer, n_lat):
    n, D = h2.shape
    top_e, gates, rank, cnt = route(logits)
    n_assign = n * TOP_K
    counts = cnt[0, :N_EXPERTS].astype(jnp.int32)
    padded = (counts + MOE_BLOCK - 1) // MOE_BLOCK * MOE_BLOCK
    pad_end = jnp.cumsum(padded)
    pad_start = pad_end - padded
    hot = top_e[:, :, None] == jnp.arange(N_EXPERTS, dtype=jnp.int32)[None, None, :]
    pos = (rank + jnp.sum(jnp.where(hot, pad_start[None, None, :], 0), axis=-1)).reshape(-1)
    n_blocks = -(-n_assign // MOE_BLOCK) + N_EXPERTS
    cap = n_blocks * MOE_BLOCK
    row_tok = jnp.full((cap,), n, jnp.int32).at[pos].set(
        jnp.arange(n_assign, dtype=jnp.int32) // TOP_K, unique_indices=True)
    blk_e = jnp.minimum(
        jnp.searchsorted(pad_end, jnp.arange(n_blocks, dtype=jnp.int32) * MOE_BLOCK, side='right'),
        N_EXPERTS - 1).astype(jnp.int32)
    n_used = (pad_end[-1:] // MOE_BLOCK).astype(jnp.int32)
    xpad = jnp.concatenate([h2, jnp.zeros((1, D), BF16)], axis=0)
    xb = xpad[row_tok]
    yb = moe_experts(xb, blk_e, n_used, wgu, bgu, w_down, b_down, layer)
    return moe_combine(yb, pos.reshape(n, TOP_K).T, gates, x1, mods, layer, n_lat)


def _rope_tables(n_lat, n_ctx):
    quarter = ROT_DIM // 4
    inv = 1.0 / (ROPE_THETA ** (jnp.arange(quarter, dtype=F32) / quarter))
    t = jnp.arange(n_lat)
    row = (t // GRID_W).astype(F32)
    col = (t % GRID_W).astype(F32)
    ang = jnp.concatenate([row[:, None] * inv, col[:, None] * inv], axis=-1)
    ang = jnp.concatenate([ang, jnp.zeros((n_ctx, ROT_DIM // 2), F32)], axis=0)
    cos, sin = jnp.cos(ang), jnp.sin(ang)
    reps = LANES // ROT_DIM
    cos_t = jnp.tile(jnp.concatenate([cos, cos], axis=-1), (1, reps))
    sin_t = jnp.tile(jnp.concatenate([-sin, sin], axis=-1), (1, reps))
    return cos_t, sin_t


def _pad_w_in(w):
    L, D, _ = w.shape
    a = MLA_Q_RANK + MLA_KV_RANK + MLA_ROPE
    return jnp.concatenate([w[..., :a], jnp.zeros((L, D, _P_DQ - a), w.dtype), w[..., a:]], axis=-1)


def _pad_w_uq(w):
    L, r, _ = w.shape
    w4 = w.reshape(L, r, MLA_HEADS, MLA_NOPE + MLA_ROPE)
    w4 = jnp.concatenate(
        [w4, jnp.zeros((L, r, MLA_HEADS, MLA_HEAD_PAD - MLA_NOPE - MLA_ROPE), w.dtype)], axis=-1)
    return w4.reshape(L, r, MLA_HEADS * MLA_HEAD_PAD)


def _row3(a, width=None, tile=1):
    a = jnp.tile(a.astype(F32), (1, tile))
    if width is not None and width > a.shape[1]:
        a = jnp.concatenate([a, jnp.zeros((a.shape[0], width - a.shape[1]), F32)], axis=1)
    return a[:, None, :]


def kernel(x, c, ctx, c_ctx, w_ada, b_ada, attn_norm, ffn_norm, w_in, mla_qa_norm, w_uq,
           mla_kva_norm, w_ukv, mla_q_gain, mla_knope_gain, mla_kpe_gain, diff_q_gain, diff_k_gain,
           diff_lambda, diff_subln, na_q_gain, na_k_gain, na_rpb, w_out, router_w, router_b,
           w_gu, b_gu, w_down, b_down):
    depth = w_in.shape[0]
    B, N, D = x.shape
    assert B == 1
    C = ctx.shape[1]
    T = N + C
    assert N % ROW_BLOCK == 0 and C % ROW_BLOCK == 0
    rows = N // GRID_W
    cos_t, sin_t = _rope_tables(N, C)
    silu = jnp.stack([jax.nn.silu(c[0]), jax.nn.silu(c_ctx)])
    silu_pad = jnp.concatenate([silu, jnp.zeros((14, D), F32)], axis=0).astype(BF16)
    n_groups = w_gu.shape[-1] // MXU_DIM

    mods = ada_all(silu_pad, w_ada)[:, :2] + b_ada[:, None, :]
    mods = mods.reshape(depth * 2 * N_MOD, 1, D)

    w_in_p = _pad_w_in(w_in).astype(BF16)
    w_uq_p = _pad_w_uq(w_uq).astype(BF16)
    w_ukv_b = w_ukv.astype(BF16)
    w_out_b = w_out.astype(BF16)
    gains = (_row3(mla_qa_norm), _row3(mla_kva_norm), _row3(mla_q_gain, MLA_HEAD_PAD),
             _row3(mla_knope_gain), _row3(mla_kpe_gain, LANES),
             _row3(diff_q_gain, tile=2), _row3(diff_k_gain, tile=2),
             _row3(na_q_gain), _row3(na_k_gain))
    attn_g = _row3(attn_norm)
    ffn_g = _row3(ffn_norm)
    subln = _row3(diff_subln)
    rw = jnp.concatenate([router_w.astype(F32),
                          jnp.zeros((depth, D, LANES - N_EXPERTS), F32)], axis=-1)
    rw_hi = rw.astype(BF16)
    rw = jnp.concatenate([rw_hi, (rw - rw_hi.astype(F32)).astype(BF16)], axis=-1)
    rb = _row3(router_b, LANES)
    bgu_all = b_gu.astype(F32).reshape(depth, N_EXPERTS, n_groups, LANES, 2)
    bgu_all = bgu_all.transpose(0, 1, 2, 4, 3).reshape(depth, N_EXPERTS, 1, n_groups * MXU_DIM)
    bd_all = b_down.astype(F32)[:, :, None, :]
    na_bias = _na_bias_tables(na_rpb, rows)

    xa = jnp.concatenate([x[0], ctx[0]], axis=0)
    for l in range(depth):
        with_ctx_out = l < depth - 1
        lam_init = 0.8 - 0.6 * math.exp(-0.3 * l)
        proj = in_proj(xa, attn_g, mods, w_in_p, l, N)
        q_a, k_a, v_a, q_b, k_b, v_b, q_c, k_c, v_c = post_proj(
            proj, cos_t, sin_t, gains, w_uq_p, w_ukv_b, l)

        lf = diff_lambda[l].astype(F32)
        lam = jnp.exp(jnp.sum(lf[0] * lf[1])) - jnp.exp(jnp.sum(lf[2] * lf[3])) + lam_init
        diff_args = (jnp.full((1, LANES), lam, F32), subln, l, 1.0 - lam_init)

        o_a = flash(q_a, k_a, v_a, heads=MLA_HEADS, dk=MLA_HEAD_PAD,
                    n_q=N, q_start=0, n_k=T, k_start=0, tq_target=512, tk_target=1280, name="mla")
        o_b = flash(q_b, k_b, v_b, heads=DIFF_HEADS, dk=2 * DIFF_QK,
                    n_q=N, q_start=0, n_k=T, k_start=0, tq_target=256, tk_target=1280,
                    diff_args=diff_args, name="diff")
        o_c = neighborhood(q_c, k_c, v_c, na_bias, N, l)
        x1, h2, logits = out_proj(o_a, o_b, o_c, xa, w_out_b, mods, ffn_g, rw, rb, l, 0, 0)

        wgu = deinterleave_gu(w_gu, l)
        if with_ctx_out:
            v_cc = _with_ones_column(v_c[N:].reshape(C, NA_HEADS, NA_DIM))
            oc_a = flash(q_a, k_a, v_a, heads=MLA_HEADS, dk=MLA_HEAD_PAD,
                         n_q=C, q_start=N, n_k=C, k_start=N, tq_target=256, tk_target=256, name="mla_ctx")
            oc_b = flash(q_b, k_b, v_b, heads=DIFF_HEADS, dk=2 * DIFF_QK,
                         n_q=C, q_start=N, n_k=C, k_start=N, tq_target=256, tk_target=256,
                         diff_args=diff_args, name="diff_ctx")
            oc_c = flash(q_c, k_c[N:], v_cc, heads=NA_HEADS, dk=NA_DIM,
                         n_q=C, q_start=N, n_k=C, k_start=0, tq_target=256, tk_target=256, name="na_ctx")
            x1c, h2c, logits_c = out_proj(oc_a, oc_b, oc_c, xa, w_out_b, mods, ffn_g, rw, rb, l, N, 1)
            x1 = jnp.concatenate([x1, x1c], axis=0)
            h2 = jnp.concatenate([h2, h2c], axis=0)
            logits = jnp.concatenate([logits, logits_c], axis=0)
        xa = moe_ffn(x1, h2, logits, wgu, bgu_all[l], w_down, bd_all, mods, l, N)
    return xa[:N][None]
```

```python
import functools
import math

import numpy as np
import jax
import jax.numpy as jnp
from jax import lax
from jax.experimental import pallas as pl
from jax.experimental.pallas import tpu as pltpu

F32 = jnp.float32
BF16 = jnp.bfloat16

GRID_W = 64
EPS = 1e-6
ROPE_THETA = 10000.0
ROT_DIM = 64
MLA_HEADS = 8
MLA_NOPE = 128
MLA_ROPE = 64
MLA_V = 128
MLA_Q_RANK = 512
MLA_KV_RANK = 256
MLA_HEAD_PAD = 256
DIFF_HEADS = 4
DIFF_QK = 64
DIFF_V = 128
NA_HEADS = 4
NA_DIM = 128
NA_KH = 8
NA_KW = 16
NA_ROWS_PER_GROUP = 4
N_EXPERTS = 32
TOP_K = 4
D_EXPERT = 768
SWIGLU_ALPHA = 1.702
SWIGLU_LIMIT = 7.0
MOE_BLOCK = 256
COMBINE_BLOCK = 128
ROW_BLOCK = 256
N_MOD = 6
NEG_BIG = -1e30
LOG2E = math.log2(math.e)

LANES = 128
MXU_DIM = 256
VMEM_LIMIT = 56 * 1024 * 1024

NT_DIMS = (((1,), (1,)), ((), ()))

_P_QA, _P_KVA, _P_KPE, _P_DQ, _P_DK, _P_DV, _P_NQ, _P_NK, _P_NV, _P_END = (
    0, 512, 768, 1024, 1536, 2048, 2560, 3072, 3584, 4096)


def _pick_tile(m, target, mult=8):
    best = None
    for t in range(mult, min(m, target) + 1, mult):
        if m % t == 0:
            best = t
    assert best is not None, (m, target)
    return best


def _params(*sem):
    return pltpu.CompilerParams(dimension_semantics=sem, vmem_limit_bytes=VMEM_LIMIT)


def _resident(block_shape, index_map):
    return pl.BlockSpec(block_shape, index_map, pipeline_mode=pl.Buffered(1))


def _mod_spec(layer, sel_fn, k):
    return lambda D: pl.BlockSpec((1, 1, D), lambda i: ((layer * 2 + sel_fn(i)) * N_MOD + k, 0, 0))


def _ada_kernel(a_ref, w_ref, o_ref):
    o_ref[0] = jnp.dot(a_ref[...], w_ref[0].astype(BF16), preferred_element_type=F32)


def ada_all(silu_pad, w_ada):
    L, D, N6 = w_ada.shape
    tn = _pick_tile(N6, 512, LANES)
    return pl.pallas_call(
        _ada_kernel,
        grid=(L, N6 // tn),
        in_specs=[pl.BlockSpec((16, D), lambda l, j: (0, 0)),
                  pl.BlockSpec((1, D, tn), lambda l, j: (l, 0, j))],
        out_specs=pl.BlockSpec((1, 16, tn), lambda l, j: (l, 0, j)),
        out_shape=jax.ShapeDtypeStruct((L, 16, N6), F32),
        compiler_params=_params("parallel", "parallel"),
        name="ada",
    )(silu_pad, w_ada)


def _norm_mm_kernel(x_ref, g_ref, sc_ref, sh_ref, w_ref, o_ref):
    x = x_ref[...]
    ms = jnp.mean(x * x, axis=-1, keepdims=True)
    h = x * lax.rsqrt(ms + EPS) * g_ref[0] * (1.0 + sc_ref[0]) + sh_ref[0]
    o_ref[...] = jnp.dot(h.astype(BF16), w_ref[0], preferred_element_type=F32)


def in_proj(xa, norm_g, mods, w_in_p, layer, n_lat):
    T, D = xa.shape
    P = w_in_p.shape[-1]
    nlb = n_lat // ROW_BLOCK
    sel = lambda i: jnp.where(i >= nlb, 1, 0)
    return pl.pallas_call(
        _norm_mm_kernel,
        grid=(T // ROW_BLOCK,),
        in_specs=[pl.BlockSpec((ROW_BLOCK, D), lambda i: (i, 0)),
                  pl.BlockSpec((1, 1, D), lambda i: (layer, 0, 0)),
                  _mod_spec(layer, sel, 1)(D),
                  _mod_spec(layer, sel, 0)(D),
                  _resident((1, D, P), lambda i: (layer, 0, 0))],
        out_specs=pl.BlockSpec((ROW_BLOCK, P), lambda i: (i, 0)),
        out_shape=jax.ShapeDtypeStruct((T, P), F32),
        compiler_params=_params("parallel"),
        name="w_in",
    )(xa, norm_g, mods, mods, w_in_p)


def _post_proj_kernel(p_ref, cos_ref, sin_ref, gqa_ref, gkva_ref, gq_ref, gkn_ref, gkpe_ref,
                      gdq_ref, gdk_ref, gnq_ref, gnk_ref, wuq_ref, wukv_ref,
                      qa_o, ka_o, va_o, qb_o, kb_o, vb_o, qc_o, kc_o, vc_o,
                      *, mla_scale, diff_scale, na_scale):
    tm = p_ref.shape[0]
    cos_t = cos_ref[...]
    sin_t = sin_ref[...]
    lane = lax.broadcasted_iota(jnp.int32, (tm, LANES), 1)
    lo = lane < ROT_DIM
    first_half = (lane & (ROT_DIM - 1)) < ROT_DIM // 2
    ones_col = jnp.where(lane == 0, 1.0, 0.0).astype(BF16)

    def rope(x):
        sw = jnp.where(first_half, pltpu.roll(x, LANES - ROT_DIM // 2, 1),
                       pltpu.roll(x, ROT_DIM // 2, 1))
        return x * cos_t + sw * sin_t

    def rinv(x, n):
        return lax.rsqrt(jnp.sum(x * x, axis=-1, keepdims=True) * (1.0 / n) + EPS)

    qa = p_ref[:, _P_QA:_P_KVA]
    qa_n = (qa * rinv(qa, MLA_Q_RANK) * gqa_ref[0]).astype(BF16)
    qf = jnp.dot(qa_n, wuq_ref[0], preferred_element_type=F32)
    g_q = gq_ref[0]
    for h in range(MLA_HEADS):
        c0 = h * MLA_HEAD_PAD
        blk = qf[:, c0:c0 + MLA_HEAD_PAD]
        r = rinv(blk, MLA_NOPE + MLA_ROPE) * mla_scale
        qa_o[:, c0:c0 + LANES] = (blk[:, :LANES] * r * g_q[:, :LANES]).astype(BF16)
        qa_o[:, c0 + LANES:c0 + 2 * LANES] = rope(blk[:, LANES:] * r * g_q[:, LANES:]).astype(BF16)

    kva = p_ref[:, _P_KVA:_P_KPE]
    kva_n = (kva * rinv(kva, MLA_KV_RANK) * gkva_ref[0]).astype(BF16)
    kv = jnp.dot(kva_n, wukv_ref[0], preferred_element_type=F32)
    kpe = p_ref[:, _P_KPE:_P_KPE + LANES]
    kpe_r = rope(kpe * rinv(kpe, MLA_ROPE) * gkpe_ref[0]).astype(BF16)
    for h in range(MLA_HEADS):
        c0 = h * MLA_HEAD_PAD
        kn = kv[:, c0:c0 + LANES]
        ka_o[:, c0:c0 + LANES] = (kn * rinv(kn, MLA_NOPE) * gkn_ref[0]).astype(BF16)
        ka_o[:, c0 + LANES:c0 + 2 * LANES] = kpe_r
        va_o[:, c0:c0 + LANES] = kv[:, c0 + LANES:c0 + 2 * LANES].astype(BF16)
        va_o[:, c0 + LANES:c0 + 2 * LANES] = ones_col

    def subhead_norm_rope(x, g_ref, scale):
        sq = x * x
        s_lo = jnp.sum(jnp.where(lo, sq, 0.0), axis=-1, keepdims=True)
        s_hi = jnp.sum(jnp.where(lo, 0.0, sq), axis=-1, keepdims=True)
        r = jnp.where(lo, lax.rsqrt(s_lo * (1.0 / DIFF_QK) + EPS),
                      lax.rsqrt(s_hi * (1.0 / DIFF_QK) + EPS))
        return (rope(x * r * g_ref[0]) * scale).astype(BF16)

    for h in range(DIFF_HEADS):
        c0 = h * LANES
        qb_o[:, c0:c0 + LANES] = subhead_norm_rope(p_ref[:, _P_DQ + c0:_P_DQ + c0 + LANES], gdq_ref, diff_scale)
        kb_o[:, c0:c0 + LANES] = subhead_norm_rope(p_ref[:, _P_DK + c0:_P_DK + c0 + LANES], gdk_ref, 1.0)
        vb_o[:, 2 * c0:2 * c0 + LANES] = p_ref[:, _P_DV + c0:_P_DV + c0 + LANES].astype(BF16)
        vb_o[:, 2 * c0 + LANES:2 * c0 + 2 * LANES] = ones_col

    for h in range(NA_HEADS):
        c0 = h * NA_DIM
        x = p_ref[:, _P_NQ + c0:_P_NQ + c0 + NA_DIM]
        qc_o[:, c0:c0 + NA_DIM] = (x * rinv(x, NA_DIM) * gnq_ref[0] * na_scale).astype(BF16)
        x = p_ref[:, _P_NK + c0:_P_NK + c0 + NA_DIM]
        kc_o[:, c0:c0 + NA_DIM] = (x * rinv(x, NA_DIM) * gnk_ref[0]).astype(BF16)
    vc_o[...] = p_ref[:, _P_NV:_P_END].astype(BF16)


def post_proj(proj, cos_t, sin_t, gains, w_uq_p, w_ukv_b, layer):
    T, P = proj.shape
    tm = ROW_BLOCK
    row = lambda w: pl.BlockSpec((tm, w), lambda i: (i, 0))
    gain_specs = [pl.BlockSpec((1, 1, g.shape[-1]), lambda i: (layer, 0, 0)) for g in gains]
    widths = (MLA_HEADS * MLA_HEAD_PAD, MLA_HEADS * MLA_HEAD_PAD, MLA_HEADS * 2 * LANES,
              DIFF_HEADS * LANES, DIFF_HEADS * LANES, DIFF_HEADS * 2 * LANES,
              NA_HEADS * NA_DIM, NA_HEADS * NA_DIM, NA_HEADS * NA_DIM)
    kern = functools.partial(
        _post_proj_kernel,
        mla_scale=(MLA_NOPE + MLA_ROPE) ** -0.5 * LOG2E,
        diff_scale=DIFF_QK ** -0.5 * LOG2E,
        na_scale=NA_DIM ** -0.5 * LOG2E)
    return pl.pallas_call(
        kern,
        grid=(T // tm,),
        in_specs=[row(P), row(LANES), row(LANES)] + gain_specs + [
            _resident((1,) + w_uq_p.shape[1:], lambda i: (layer, 0, 0)),
            _resident((1,) + w_ukv_b.shape[1:], lambda i: (layer, 0, 0))],
        out_specs=[row(w) for w in widths],
        out_shape=[jax.ShapeDtypeStruct((T, w), BF16) for w in widths],
        compiler_params=_params("parallel"),
        name="post_proj",
    )(proj, cos_t, sin_t, *gains, w_uq_p, w_ukv_b)


def _flash_kernel(*refs, tk, n_chunks, diff, lam_scale):
    if diff:
        q_ref, k_ref, v_ref, lam_ref, g_ref, o_ref, q_scr, s_scr, acc_scr = refs
    else:
        q_ref, k_ref, v_ref, o_ref, q_scr, s_scr, acc_scr = refs
    q = q_ref[...]
    tq = q.shape[0]
    if diff:
        lane = lax.broadcasted_iota(jnp.int32, q.shape, 1)
        zero = jnp.zeros_like(q)
        q_scr[:tq, :] = jnp.where(lane < DIFF_QK, q, zero)
        q_scr[tq:, :] = jnp.where(lane >= DIFF_QK, q, zero)
    else:
        q_scr[...] = q
    rows = q_scr.shape[0]
    acc_scr[...] = jnp.zeros_like(acc_scr)

    def qk(c, slot):
        start = pl.multiple_of(c * tk, tk)
        s_scr[slot] = lax.dot_general(q_scr[...], k_ref[pl.ds(start, tk), :], NT_DIMS,
                                      preferred_element_type=F32)

    def consume(c, slot, m):
        s = s_scr[slot]
        m_new = jnp.maximum(m, jnp.max(s, axis=-1, keepdims=True))
        alpha = jnp.exp2(m - m_new)
        p = jnp.exp2(s - m_new).astype(BF16)
        start = pl.multiple_of(c * tk, tk)
        acc_scr[...] = alpha * acc_scr[...] + jnp.dot(
            p, v_ref[pl.ds(start, tk), :], preferred_element_type=F32)
        return m_new

    def pair(i, m):
        c = 2 * i
        qk(c + 1, 1)
        m = consume(c, 0, m)
        qk(c + 2, 0)
        return consume(c + 1, 1, m)

    m = jnp.full((rows, 1), -jnp.inf, F32)
    qk(0, 0)
    n_pairs = (n_chunks - 1) // 2
    if n_pairs > 0:
        m = lax.fori_loop(0, n_pairs, pair, m, unroll=True)
    if n_chunks % 2 == 1:
        m = consume(n_chunks - 1, 0, m)
    else:
        qk(n_chunks - 1, 1)
        m = consume(n_chunks - 2, 0, m)
        m = consume(n_chunks - 1, 1, m)
    acc = acc_scr[...]
    o = acc[:, :LANES] / acc[:, LANES:LANES + 1]
    if diff:
        o = o[:tq] - lam_ref[...] * o[tq:]
        ms = jnp.mean(o * o, axis=-1, keepdims=True)
        o = o * lax.rsqrt(ms + EPS) * g_ref[0] * lam_scale
    o_ref[...] = o.astype(o_ref.dtype)


def flash(q, k, v, *, heads, dk, n_q, q_start, n_k, k_start, tq_target, tk_target,
          diff_args=None, name="flash"):
    tq = _pick_tile(math.gcd(n_q, q_start) if q_start else n_q, tq_target)
    tk = _pick_tile(n_k, tk_target, LANES)
    assert k_start % n_k == 0
    q_blk0 = q_start // tq
    k_blk0 = k_start // n_k
    vw = 2 * LANES
    in_specs = [pl.BlockSpec((tq, dk), lambda h, i: (q_blk0 + i, h)),
                pl.BlockSpec((n_k, dk), lambda h, i: (k_blk0, h)),
                pl.BlockSpec((n_k, vw), lambda h, i: (k_blk0, h))]
    args = [q, k, v]
    diff = diff_args is not None
    lam_scale = 1.0
    if diff:
        lam_row, subln, layer, lam_scale = diff_args
        in_specs += [pl.BlockSpec((1, LANES), lambda h, i: (0, 0)),
                     pl.BlockSpec((1, 1, LANES), lambda h, i: (layer, 0, 0))]
        args += [lam_row, subln]
    rows = 2 * tq if diff else tq
    kern = functools.partial(_flash_kernel, tk=tk, n_chunks=n_k // tk, diff=diff,
                             lam_scale=lam_scale)
    return pl.pallas_call(
        kern,
        grid=(heads, n_q // tq),
        in_specs=in_specs,
        out_specs=pl.BlockSpec((tq, LANES), lambda h, i: (i, h)),
        out_shape=jax.ShapeDtypeStruct((n_q, heads * LANES), BF16),
        scratch_shapes=[pltpu.VMEM((rows, dk), BF16),
                        pltpu.VMEM((2, rows, tk), F32),
                        pltpu.VMEM((rows, vw), F32)],
        compiler_params=_params("parallel", "arbitrary"),
        name=name,
    )(*args)


def _with_ones_column(v3):
    T, H, d = v3.shape
    pad = jnp.zeros((T, H, 2 * LANES - d), BF16).at[:, :, 0].set(1.0)
    return jnp.concatenate([v3.astype(BF16), pad], axis=-1).reshape(T, H * 2 * LANES)


def _na_kernel(q_ref, k0_ref, k1_ref, k2_ref, v0_ref, v1_ref, v2_ref, kc_ref, vc_ref, bias_ref,
               o_ref):
    k_refs = (k0_ref, k1_ref, k2_ref, kc_ref)
    v_refs = (v0_ref, v1_ref, v2_ref, vc_ref)
    blk = k0_ref.shape[0]
    for h in range(NA_HEADS):
        cols = slice(h * NA_DIM, (h + 1) * NA_DIM)
        q = q_ref[:, cols]
        s_parts = []
        for b in range(4):
            s = lax.dot_general(q, k_refs[b][:, cols], NT_DIMS, preferred_element_type=F32)
            if b < 3:
                s = s + bias_ref[0, 0, h, :, b * blk:(b + 1) * blk]
            s_parts.append(s)
        m = s_parts[3].max(axis=-1, keepdims=True)
        for b in range(3):
            m = jnp.maximum(m, s_parts[b].max(axis=-1, keepdims=True))
        l = jnp.zeros_like(m)
        acc = jnp.zeros((q.shape[0], NA_DIM), F32)
        for b in range(4):
            p = jnp.exp2(s_parts[b] - m)
            l = l + p.sum(axis=-1, keepdims=True)
            acc = acc + jnp.dot(p.astype(BF16), v_refs[b][:, cols], preferred_element_type=F32)
        o_ref[:, cols] = (acc / l).astype(o_ref.dtype)


def _na_bias_tables(rpb, rows):
    R = NA_ROWS_PER_GROUP
    G = rows // R
    assert rows % R == 0 and G >= 3 and rows >= NA_KH and NA_KH == 2 * R
    n_rr, n_cr = 2 * NA_KH - 1, 2 * NA_KW - 1
    qc = np.arange(GRID_W)[:, None]
    kc = np.arange(GRID_W)[None, :]
    wc = np.clip(qc - NA_KW // 2, 0, GRID_W - NA_KW)
    valid_c = (kc >= wc) & (kc < wc + NA_KW)
    col_rel = np.clip(kc - qc + NA_KW - 1, 0, n_cr - 1).reshape(-1)
    c_sel = (np.arange(n_cr)[:, None] == col_rel[None, :]).astype(np.float32)
    tabs = []
    for g in (0, 1, G - 1):
        j = np.arange(R)[:, None, None]
        b = np.arange(3)[None, :, None]
        kr = np.arange(R)[None, None, :]
        r = R * g + j
        rs = np.clip(r - NA_KH // 2, 0, rows - NA_KH)
        keyrow = R * (g - 1 + b) + kr
        valid_r = np.broadcast_to((keyrow >= rs) & (keyrow < rs + NA_KH), (R, 3, R))
        row_rel = np.broadcast_to(np.clip(keyrow - r + NA_KH - 1, 0, n_rr - 1), (R, 3, R)).reshape(-1)
        r_sel = (row_rel[:, None] == np.arange(n_rr)[None, :]).astype(np.float32)
        t = jnp.einsum('ar,lhrc,cq->lhaq', r_sel, rpb.astype(F32), c_sel,
                       precision=lax.Precision.HIGHEST)
        L, H = t.shape[:2]
        t = t.reshape(L, H, R, 3, R, GRID_W, GRID_W).transpose(0, 1, 2, 5, 3, 4, 6)
        valid = valid_r[:, None, :, :, None] & valid_c[None, :, None, None, :]
        t = jnp.where(valid[None, None], t * LOG2E, NEG_BIG)
        tabs.append(t.reshape(L, H, R * GRID_W, 3 * R * GRID_W))
    return jnp.stack(tabs, axis=1)


def neighborhood(q, k, v, bias_tabs, n_lat, layer):
    blk = NA_ROWS_PER_GROUP * GRID_W
    G = n_lat // blk
    n_ctx = q.shape[0] - n_lat
    assert n_ctx == blk, "context length must equal one key block"
    W = NA_HEADS * NA_DIM

    def kv_spec(off):
        return pl.BlockSpec((blk, W), lambda g: (jnp.clip(g + off, 0, G - 1), 0))

    ctx_spec = pl.BlockSpec((blk, W), lambda g: (G, 0))
    bias_spec = pl.BlockSpec(
        (1, 1, NA_HEADS, blk, 3 * blk),
        lambda g: (layer, jnp.where(g == 0, 0, jnp.where(g == G - 1, 2, 1)), 0, 0, 0))
    return pl.pallas_call(
        _na_kernel,
        grid=(G,),
        in_specs=[pl.BlockSpec((blk, W), lambda g: (g, 0)),
                  kv_spec(-1), kv_spec(0), kv_spec(1),
                  kv_spec(-1), kv_spec(0), kv_spec(1),
                  ctx_spec, ctx_spec, bias_spec],
        out_specs=pl.BlockSpec((blk, W), lambda g: (g, 0)),
        out_shape=jax.ShapeDtypeStruct((n_lat, W), BF16),
        compiler_params=_params("parallel"),
        name="na",
    )(q, k, k, k, v, v, v, k, v, bias_tabs)


def _out_proj_kernel(oa_ref, ob_ref, oc_ref, x_ref, w_ref, g1_ref, gn_ref, sc_ref, sh_ref,
                     rw_ref, rb_ref, x1_o, h2_o, lg_o):
    wa = oa_ref.shape[1]
    wb = ob_ref.shape[1]
    o = jnp.dot(oa_ref[...], w_ref[0, 0:wa, :], preferred_element_type=F32)
    o = o + jnp.dot(ob_ref[...], w_ref[0, wa:wa + wb, :], preferred_element_type=F32)
    o = o + jnp.dot(oc_ref[...], w_ref[0, wa + wb:, :], preferred_element_type=F32)
    x1 = x_ref[...] + g1_ref[0] * o
    x1_o[...] = x1
    ms = jnp.mean(x1 * x1, axis=-1, keepdims=True)
    h2 = x1 * lax.rsqrt(ms + EPS) * gn_ref[0] * (1.0 + sc_ref[0]) + sh_ref[0]
    h2_o[...] = h2
    h2_hi = h2.astype(BF16)
    h2_lo = (h2 - h2_hi.astype(F32)).astype(BF16)
    a = (jnp.dot(h2_hi, rw_ref[0], preferred_element_type=F32)
         + jnp.dot(h2_lo, rw_ref[0], preferred_element_type=F32))
    lg_o[...] = a[:, :LANES] + a[:, LANES:] + rb_ref[0]


def out_proj(o_a, o_b, o_c, xa, w_out_b, mods, ffn_g, rw, rb, layer, row_start, stream):
    M = o_a.shape[0]
    D = xa.shape[1]
    tm = ROW_BLOCK
    blk0 = row_start // tm
    row = lambda a: pl.BlockSpec((tm, a.shape[1]), lambda i: (i, 0))
    sel = lambda i: stream
    return pl.pallas_call(
        _out_proj_kernel,
        grid=(M // tm,),
        in_specs=[row(o_a), row(o_b), row(o_c),
                  pl.BlockSpec((tm, D), lambda i: (blk0 + i, 0)),
                  _resident((1,) + w_out_b.shape[1:], lambda i: (layer, 0, 0)),
                  _mod_spec(layer, sel, 2)(D),
                  pl.BlockSpec((1, 1, D), lambda i: (layer, 0, 0)),
                  _mod_spec(layer, sel, 4)(D),
                  _mod_spec(layer, sel, 3)(D),
                  _resident((1, D, 2 * LANES), lambda i: (layer, 0, 0)),
                  pl.BlockSpec((1, 1, LANES), lambda i: (layer, 0, 0))],
        out_specs=[pl.BlockSpec((tm, D), lambda i: (i, 0)),
                   pl.BlockSpec((tm, D), lambda i: (i, 0)),
                   pl.BlockSpec((tm, LANES), lambda i: (i, 0))],
        out_shape=[jax.ShapeDtypeStruct((M, D), F32),
                   jax.ShapeDtypeStruct((M, D), F32),
                   jax.ShapeDtypeStruct((M, LANES), F32)],
        compiler_params=_params("parallel"),
        name="w_out",
    )(o_a, o_b, o_c, xa, w_out_b, mods, ffn_g, mods, mods, rw, rb)


def _deinterleave_matrix():
    p = np.zeros((MXU_DIM, MXU_DIM), np.float32)
    i = np.arange(MXU_DIM // 2)
    p[2 * i, i] = 1.0
    p[2 * i + 1, MXU_DIM // 2 + i] = 1.0
    return jnp.asarray(p, BF16)


def _deint_kernel(w_ref, p_ref, o_ref):
    for c in range(w_ref.shape[-1] // MXU_DIM):
        cols = slice(c * MXU_DIM, (c + 1) * MXU_DIM)
        w = w_ref[0, 0, :, cols].astype(BF16)
        o_ref[0, :, cols] = jnp.dot(w, p_ref[...], preferred_element_type=F32).astype(BF16)


def deinterleave_gu(w_gu, layer):
    _, E, D, F2 = w_gu.shape
    tr = _pick_tile(D, ROW_BLOCK)
    return pl.pallas_call(
        _deint_kernel,
        grid=(E, D // tr),
        in_specs=[pl.BlockSpec((1, 1, tr, F2), lambda e, j: (layer, e, j, 0)),
                  pl.BlockSpec((MXU_DIM, MXU_DIM), lambda e, j: (0, 0))],
        out_specs=pl.BlockSpec((1, tr, F2), lambda e, j: (e, j, 0)),
        out_shape=jax.ShapeDtypeStruct((E, D, F2), BF16),
        compiler_params=_params("parallel", "parallel"),
        name="deint_gu",
    )(w_gu, _deinterleave_matrix())


def _moe_kernel(blk_e_ref, n_used_ref, x_ref, wgu_ref, bgu_ref, wd_ref, bd_ref, o_ref, wd_scr):
    i = pl.program_id(0)
    e = blk_e_ref[i]
    e_prev = blk_e_ref[jnp.maximum(i - 1, 0)]

    @pl.when((i == 0) | (e != e_prev))
    def _():
        wd_scr[...] = wd_ref[0, 0].astype(BF16)

    @pl.when(i < n_used_ref[0])
    def _():
        x = x_ref[...].astype(BF16)
        gu = jnp.dot(x, wgu_ref[0], preferred_element_type=F32) + bgu_ref[0]
        acts = []
        for c in range(gu.shape[1] // MXU_DIM):
            g = gu[:, c * MXU_DIM:c * MXU_DIM + LANES]
            u = gu[:, c * MXU_DIM + LANES:(c + 1) * MXU_DIM]
            glu = jnp.minimum(g, SWIGLU_LIMIT)
            lin = jnp.clip(u, -SWIGLU_LIMIT, SWIGLU_LIMIT)
            acts.append((glu * jax.nn.sigmoid(SWIGLU_ALPHA * glu) * (lin + 1.0)).astype(BF16))
        act = jnp.concatenate(acts, axis=1)
        o_ref[...] = jnp.dot(act, wd_scr[...], preferred_element_type=F32) + bd_ref[0, 0]

    @pl.when(i >= n_used_ref[0])
    def _():
        o_ref[...] = jnp.zeros_like(o_ref)


def moe_experts(xb, blk_e, n_used, wgu, bgu, w_down, b_down, layer):
    cap, D = xb.shape
    n_blocks = cap // MOE_BLOCK
    F2 = wgu.shape[-1]
    F = w_down.shape[2]
    grid_spec = pltpu.PrefetchScalarGridSpec(
        num_scalar_prefetch=2,
        grid=(n_blocks,),
        in_specs=[pl.BlockSpec((MOE_BLOCK, D), lambda i, e, n: (jnp.minimum(i, n[0] - 1), 0)),
                  pl.BlockSpec((1, D, F2), lambda i, e, n: (e[i], 0, 0)),
                  pl.BlockSpec((1, 1, F2), lambda i, e, n: (e[i], 0, 0)),
                  pl.BlockSpec((1, 1, F, D), lambda i, e, n: (layer, e[i], 0, 0)),
                  pl.BlockSpec((1, 1, 1, D), lambda i, e, n: (layer, e[i], 0, 0))],
        out_specs=pl.BlockSpec((MOE_BLOCK, D), lambda i, e, n: (i, 0)),
        scratch_shapes=[pltpu.VMEM((F, D), BF16)],
    )
    return pl.pallas_call(
        _moe_kernel,
        grid_spec=grid_spec,
        out_shape=jax.ShapeDtypeStruct((cap, D), F32),
        compiler_params=_params("arbitrary"),
        name="moe_experts",
    )(blk_e, n_used, xb, wgu, bgu, w_down, b_down)


def _combine_kernel(pc0, pc1, pc2, pc3, pn0, pn1, pn2, pn3, g_ref, x_ref, g2_ref, yb_hbm,
                    o_ref, buf, sem):
    i = pl.program_id(0)
    nb = pl.num_programs(0)
    tb = o_ref.shape[0]
    slot = i % 2

    def issue(pos_refs, s):
        for k in range(TOP_K):
            def body(t, carry, k=k):
                r = pos_refs[k][t]
                pltpu.make_async_copy(yb_hbm.at[pl.ds(r, 1), :],
                                      buf.at[s, pl.ds(k * tb + t, 1), :], sem.at[s]).start()
                return carry
            lax.fori_loop(0, tb, body, 0, unroll=8)

    @pl.when(i == 0)
    def _():
        issue((pc0, pc1, pc2, pc3), 0)

    @pl.when(i + 1 < nb)
    def _():
        issue((pn0, pn1, pn2, pn3), 1 - slot)

    pltpu.make_async_copy(yb_hbm.at[pl.ds(0, TOP_K * tb), :], buf.at[slot], sem.at[slot]).wait()
    y = jnp.zeros(o_ref.shape, F32)
    for k in range(TOP_K):
        y = y + g_ref[:, k:k + 1] * buf[slot, k * tb:(k + 1) * tb, :]
    o_ref[...] = x_ref[...] + g2_ref[0] * y


def moe_combine(yb, pos, gates, x1, mods, layer, n_lat):
    n, D = x1.shape
    tb = COMBINE_BLOCK
    nb = n // tb
    nlb = n_lat // tb
    sel = lambda i: jnp.where(i >= nlb, 1, 0)
    cur = pl.BlockSpec((tb,), lambda i: (i,), memory_space=pltpu.SMEM)
    nxt = pl.BlockSpec((tb,), lambda i: (jnp.minimum(i + 1, nb - 1),), memory_space=pltpu.SMEM)
    return pl.pallas_call(
        _combine_kernel,
        grid=(nb,),
        in_specs=[cur] * TOP_K + [nxt] * TOP_K + [
            pl.BlockSpec((tb, TOP_K), lambda i: (i, 0)),
            pl.BlockSpec((tb, D), lambda i: (i, 0)),
            _mod_spec(layer, sel, 5)(D),
            pl.BlockSpec(memory_space=pl.ANY)],
        out_specs=pl.BlockSpec((tb, D), lambda i: (i, 0)),
        out_shape=jax.ShapeDtypeStruct((n, D), F32),
        scratch_shapes=[pltpu.VMEM((2, TOP_K * tb, D), F32),
                        pltpu.SemaphoreType.DMA((2,))],
        compiler_params=_params("arbitrary"),
        name="moe_combine",
    )(pos[0], pos[1], pos[2], pos[3], pos[0], pos[1], pos[2], pos[3], gates, x1, mods, yb)


def _route_kernel(lg_ref, e_o, g_o, r_o, cnt_o, run_scr):
    i = pl.program_id(0)

    @pl.when(i == 0)
    def _():
        run_scr[...] = jnp.zeros_like(run_scr)

    tb = lg_ref.shape[0]
    lane = lax.broadcasted_iota(jnp.int32, (tb, LANES), 1)
    x = jnp.where(lane < N_EXPERTS, lg_ref[...], -jnp.inf)
    vals, hots = [], []
    for k in range(TOP_K):
        m = jnp.max(x, axis=-1, keepdims=True)
        idx = jnp.min(jnp.where(x == m, lane, LANES), axis=-1, keepdims=True)
        hot = lane == idx
        e_o[:, k:k + 1] = idx
        vals.append(m)
        hots.append(hot)
        x = jnp.where(hot, -jnp.inf, x)
    exps = [jnp.exp(v - vals[0]) for v in vals]
    denom = exps[0] + exps[1] + exps[2] + exps[3]
    for k in range(TOP_K):
        g_o[:, k:k + 1] = exps[k] / denom
    picked = (hots[0] | hots[1] | hots[2] | hots[3]).astype(BF16)
    r_idx = lax.broadcasted_iota(jnp.int32, (tb, tb), 0)
    c_idx = lax.broadcasted_iota(jnp.int32, (tb, tb), 1)
    before = (c_idx < r_idx).astype(BF16)
    seen = jnp.dot(before, picked, preferred_element_type=F32) + run_scr[...]
    for k in range(TOP_K):
        rank = jnp.sum(jnp.where(hots[k], seen, 0.0), axis=-1, keepdims=True)
        r_o[:, k:k + 1] = rank.astype(jnp.int32)
    run_scr[...] = run_scr[...] + jnp.sum(picked.astype(F32), axis=0, keepdims=True)
    cnt_o[...] = run_scr[...]


def route(logits):
    n = logits.shape[0]
    tb = ROW_BLOCK
    out = lambda: pl.BlockSpec((tb, TOP_K), lambda i: (i, 0))
    return pl.pallas_call(
        _route_kernel,
        grid=(n // tb,),
        in_specs=[pl.BlockSpec((tb, LANES), lambda i: (i, 0))],
        out_specs=[out(), out(), out(), pl.BlockSpec((1, LANES), lambda i: (0, 0))],
        out_shape=[jax.ShapeDtypeStruct((n, TOP_K), jnp.int32),
                   jax.ShapeDtypeStruct((n, TOP_K), F32),
                   jax.ShapeDtypeStruct((n, TOP_K), jnp.int32),
                   jax.ShapeDtypeStruct((1, LANES), F32)],
        scratch_shapes=[pltpu.VMEM((1, LANES), F32)],
        compiler_params=_params("arbitrary"),
        name="route",
    )(logits)


def _dispatch_kernel(pe_ref, cnt_ref, p0, p1, p2, p3, x_ref, xb_hbm, src, zeros, sem, zsem):
    i = pl.program_id(0)
    nb = pl.num_programs(0)
    tb = x_ref.shape[0]
    slot = i % 2
    pos_refs = (p0, p1, p2, p3)

    def zero_copy(row0):
        return pltpu.make_async_copy(zeros, xb_hbm.at[pl.ds(row0, MOE_BLOCK), :], zsem)

    @pl.when(i == 0)
    def _():
        zeros[...] = jnp.zeros_like(zeros)
        for e in range(N_EXPERTS):
            @pl.when(cnt_ref[e] > 0)
            def _(e=e):
                zero_copy(pl.multiple_of(pe_ref[e] - MOE_BLOCK, MOE_BLOCK)).start()
        n_used = pe_ref[N_EXPERTS - 1] // MOE_BLOCK
        n_blocks = xb_hbm.shape[0] // MOE_BLOCK

        def fill(b, carry):
            zero_copy(pl.multiple_of(b * MOE_BLOCK, MOE_BLOCK)).start()
            return carry

        def drain(b, carry):
            zero_copy(0).wait()
            return carry

        lax.fori_loop(n_used, n_blocks, fill, 0)
        lax.fori_loop(n_used, n_blocks, drain, 0)
        for e in range(N_EXPERTS):
            @pl.when(cnt_ref[e] > 0)
            def _():
                zero_copy(0).wait()

    def wait_slot(s):
        for _ in range(TOP_K):
            pltpu.make_async_copy(src.at[s], xb_hbm.at[pl.ds(0, tb), :], sem.at[s]).wait()

    @pl.when(i >= 2)
    def _():
        wait_slot(slot)

    src[slot] = x_ref[...]
    for k in range(TOP_K):
        def body(t, carry, k=k):
            r = pos_refs[k][t]
            pltpu.make_async_copy(src.at[slot, pl.ds(t, 1), :], xb_hbm.at[pl.ds(r, 1), :],
                                  sem.at[slot]).start()
            return carry
        lax.fori_loop(0, tb, body, 0, unroll=8)

    @pl.when(i == nb - 1)
    def _():
        wait_slot(slot)

    @pl.when((i == nb - 1) & (i >= 1))
    def _():
        wait_slot(1 - slot)


def moe_dispatch(h2, pos, pad_end, counts, cap):
    n, D = h2.shape
    tb = COMBINE_BLOCK
    grid_spec = pltpu.PrefetchScalarGridSpec(
        num_scalar_prefetch=2,
        grid=(n // tb,),
        in_specs=[pl.BlockSpec((tb,), lambda i, pe, c: (i,), memory_space=pltpu.SMEM)] * TOP_K + [
            pl.BlockSpec((tb, D), lambda i, pe, c: (i, 0))],
        out_specs=pl.BlockSpec(memory_space=pl.ANY),
        scratch_shapes=[pltpu.VMEM((2, tb, D), F32),
                        pltpu.VMEM((MOE_BLOCK, D), F32),
                        pltpu.SemaphoreType.DMA((2,)),
                        pltpu.SemaphoreType.DMA(())],
    )
    return pl.pallas_call(
        _dispatch_kernel,
        grid_spec=grid_spec,
        out_shape=jax.ShapeDtypeStruct((cap, D), F32),
        compiler_params=_params("arbitrary"),
        name="moe_dispatch",
    )(pad_end, counts, pos[0], pos[1], pos[2], pos[3], h2)


def moe_ffn(x1, h2, logits, wgu, bgu, w_down, b_down, mods, layer, n_lat):
    n, D = h2.shape
    top_e, gates, rank, cnt = route(logits)
    n_assign = n * TOP_K
    counts = cnt[0, :N_EXPERTS].astype(jnp.int32)
    padded = (counts + MOE_BLOCK - 1) // MOE_BLOCK * MOE_BLOCK
    pad_end = jnp.cumsum(padded)
    pad_start = pad_end - padded
    hot = top_e[:, :, None] == jnp.arange(N_EXPERTS, dtype=jnp.int32)[None, None, :]
    pos = (rank + jnp.sum(jnp.where(hot, pad_start[None, None, :], 0), axis=-1)).T
    n_blocks = -(-n_assign // MOE_BLOCK) + N_EXPERTS
    cap = n_blocks * MOE_BLOCK
    blk_row0 = jnp.arange(n_blocks, dtype=jnp.int32) * MOE_BLOCK
    blk_e = jnp.minimum(jnp.sum((pad_end[None, :] <= blk_row0[:, None]).astype(jnp.int32), axis=1),
                        N_EXPERTS - 1)
    n_used = pad_end[-1:] // MOE_BLOCK
    xb = moe_dispatch(h2, pos, pad_end, counts, cap)
    yb = moe_experts(xb, blk_e, n_used, wgu, bgu, w_down, b_down, layer)
    return moe_combine(yb, pos, gates, x1, mods, layer, n_lat)


def _rope_tables(n_lat, n_ctx):
    quarter = ROT_DIM // 4
    inv = 1.0 / (ROPE_THETA ** (jnp.arange(quarter, dtype=F32) / quarter))
    t = jnp.arange(n_lat)
    row = (t // GRID_W).astype(F32)
    col = (t % GRID_W).astype(F32)
    ang = jnp.concatenate([row[:, None] * inv, col[:, None] * inv], axis=-1)
    ang = jnp.concatenate([ang, jnp.zeros((n_ctx, ROT_DIM // 2), F32)], axis=0)
    cos, sin = jnp.cos(ang), jnp.sin(ang)
    reps = LANES // ROT_DIM
    cos_t = jnp.tile(jnp.concatenate([cos, cos], axis=-1), (1, reps))
    sin_t = jnp.tile(jnp.concatenate([-sin, sin], axis=-1), (1, reps))
    return cos_t, sin_t


def _pad_w_in(w):
    L, D, _ = w.shape
    a = MLA_Q_RANK + MLA_KV_RANK + MLA_ROPE
    return jnp.concatenate([w[..., :a], jnp.zeros((L, D, _P_DQ - a), w.dtype), w[..., a:]], axis=-1)


def _pad_w_uq(w):
    L, r, _ = w.shape
    w4 = w.reshape(L, r, MLA_HEADS, MLA_NOPE + MLA_ROPE)
    w4 = jnp.concatenate(
        [w4, jnp.zeros((L, r, MLA_HEADS, MLA_HEAD_PAD - MLA_NOPE - MLA_ROPE), w.dtype)], axis=-1)
    return w4.reshape(L, r, MLA_HEADS * MLA_HEAD_PAD)


def _row3(a, width=None, tile=1):
    a = jnp.tile(a.astype(F32), (1, tile))
    if width is not None and width > a.shape[1]:
        a = jnp.concatenate([a, jnp.zeros((a.shape[0], width - a.shape[1]), F32)], axis=1)
    return a[:, None, :]


def kernel(x, c, ctx, c_ctx, w_ada, b_ada, attn_norm, ffn_norm, w_in, mla_qa_norm, w_uq,
           mla_kva_norm, w_ukv, mla_q_gain, mla_knope_gain, mla_kpe_gain, diff_q_gain, diff_k_gain,
           diff_lambda, diff_subln, na_q_gain, na_k_gain, na_rpb, w_out, router_w, router_b,
           w_gu, b_gu, w_down, b_down):
    depth = w_in.shape[0]
    B, N, D = x.shape
    assert B == 1
    C = ctx.shape[1]
    T = N + C
    assert N % ROW_BLOCK == 0 and C % ROW_BLOCK == 0
    rows = N // GRID_W
    cos_t, sin_t = _rope_tables(N, C)
    silu = jnp.stack([jax.nn.silu(c[0]), jax.nn.silu(c_ctx)])
    silu_pad = jnp.concatenate([silu, jnp.zeros((14, D), F32)], axis=0).astype(BF16)
    n_groups = w_gu.shape[-1] // MXU_DIM

    mods = ada_all(silu_pad, w_ada)[:, :2] + b_ada[:, None, :]
    mods = mods.reshape(depth * 2 * N_MOD, 1, D)

    w_in_p = _pad_w_in(w_in).astype(BF16)
    w_uq_p = _pad_w_uq(w_uq).astype(BF16)
    w_ukv_b = w_ukv.astype(BF16)
    w_out_b = w_out.astype(BF16)
    gains = (_row3(mla_qa_norm), _row3(mla_kva_norm), _row3(mla_q_gain, MLA_HEAD_PAD),
             _row3(mla_knope_gain), _row3(mla_kpe_gain, LANES),
             _row3(diff_q_gain, tile=2), _row3(diff_k_gain, tile=2),
             _row3(na_q_gain), _row3(na_k_gain))
    attn_g = _row3(attn_norm)
    ffn_g = _row3(ffn_norm)
    subln = _row3(diff_subln)
    rw = jnp.concatenate([router_w.astype(F32),
                          jnp.zeros((depth, D, LANES - N_EXPERTS), F32)], axis=-1)
    rw_hi = rw.astype(BF16)
    rw = jnp.concatenate([rw_hi, (rw - rw_hi.astype(F32)).astype(BF16)], axis=-1)
    rb = _row3(router_b, LANES)
    bgu_all = b_gu.astype(F32).reshape(depth, N_EXPERTS, n_groups, LANES, 2)
    bgu_all = bgu_all.transpose(0, 1, 2, 4, 3).reshape(depth, N_EXPERTS, 1, n_groups * MXU_DIM)
    bd_all = b_down.astype(F32)[:, :, None, :]
    na_bias = _na_bias_tables(na_rpb, rows)

    xa = jnp.concatenate([x[0], ctx[0]], axis=0)
    for l in range(depth):
        with_ctx_out = l < depth - 1
        lam_init = 0.8 - 0.6 * math.exp(-0.3 * l)
        proj = in_proj(xa, attn_g, mods, w_in_p, l, N)
        q_a, k_a, v_a, q_b, k_b, v_b, q_c, k_c, v_c = post_proj(
            proj, cos_t, sin_t, gains, w_uq_p, w_ukv_b, l)

        lf = diff_lambda[l].astype(F32)
        lam = jnp.exp(jnp.sum(lf[0] * lf[1])) - jnp.exp(jnp.sum(lf[2] * lf[3])) + lam_init
        diff_args = (jnp.full((1, LANES), lam, F32), subln, l, 1.0 - lam_init)

        o_a = flash(q_a, k_a, v_a, heads=MLA_HEADS, dk=MLA_HEAD_PAD,
                    n_q=N, q_start=0, n_k=T, k_start=0, tq_target=512, tk_target=1280, name="mla")
        o_b = flash(q_b, k_b, v_b, heads=DIFF_HEADS, dk=2 * DIFF_QK,
                    n_q=N, q_start=0, n_k=T, k_start=0, tq_target=256, tk_target=1280,
                    diff_args=diff_args, name="diff")
        o_c = neighborhood(q_c, k_c, v_c, na_bias, N, l)
        x1, h2, logits = out_proj(o_a, o_b, o_c, xa, w_out_b, mods, ffn_g, rw, rb, l, 0, 0)

        wgu = deinterleave_gu(w_gu, l)
        if with_ctx_out:
            v_cc = _with_ones_column(v_c[N:].reshape(C, NA_HEADS, NA_DIM))
            oc_a = flash(q_a, k_a, v_a, heads=MLA_HEADS, dk=MLA_HEAD_PAD,
                         n_q=C, q_start=N, n_k=C, k_start=N, tq_target=256, tk_target=256, name="mla_ctx")
            oc_b = flash(q_b, k_b, v_b, heads=DIFF_HEADS, dk=2 * DIFF_QK,
                         n_q=C, q_start=N, n_k=C, k_start=N, tq_target=256, tk_target=256,
                         diff_args=diff_args, name="diff_ctx")
            oc_c = flash(q_c, k_c[N:], v_cc, heads=NA_HEADS, dk=NA_DIM,
                         n_q=C, q_start=N, n_k=C, k_start=0, tq_target=256, tk_target=256, name="na_ctx")
            x1c, h2c, logits_c = out_proj(oc_a, oc_b, oc_c, xa, w_out_b, mods, ffn_g, rw, rb, l, N, 1)
            x1 = jnp.concatenate([x1, x1c], axis=0)
            h2 = jnp.concatenate([h2, h2c], axis=0)
            logits = jnp.concatenate([logits, logits_c], axis=0)
        xa = moe_ffn(x1, h2, logits, wgu, bgu_all[l], w_down, bd_all, mods, l, N)
    return xa[:N][None]
```

```python
import functools
import math

import numpy as np
import jax
import jax.numpy as jnp
from jax import lax
from jax.experimental import pallas as pl
from jax.experimental.pallas import tpu as pltpu

F32 = jnp.float32
BF16 = jnp.bfloat16

GRID_W = 64
EPS = 1e-6
ROPE_THETA = 10000.0
ROT_DIM = 64
MLA_HEADS = 8
MLA_NOPE = 128
MLA_ROPE = 64
MLA_V = 128
MLA_Q_RANK = 512
MLA_KV_RANK = 256
MLA_HEAD_PAD = 256
DIFF_HEADS = 4
DIFF_QK = 64
DIFF_V = 128
NA_HEADS = 4
NA_DIM = 128
NA_KH = 8
NA_KW = 16
NA_ROWS_PER_GROUP = 4
N_EXPERTS = 32
TOP_K = 4
D_EXPERT = 768
SWIGLU_ALPHA = 1.702
SWIGLU_LIMIT = 7.0
MOE_BLOCK = 256
COMBINE_BLOCK = 128
ROW_BLOCK = 256
N_MOD = 6
NEG_BIG = -1e30
LOG2E = math.log2(math.e)

LANES = 128
MXU_DIM = 256
VMEM_LIMIT = 56 * 1024 * 1024

NT_DIMS = (((1,), (1,)), ((), ()))

_P_QA, _P_KVA, _P_KPE, _P_DQ, _P_DK, _P_DV, _P_NQ, _P_NK, _P_NV, _P_END = (
    0, 512, 768, 1024, 1536, 2048, 2560, 3072, 3584, 4096)


def _pick_tile(m, target, mult=8):
    best = None
    for t in range(mult, min(m, target) + 1, mult):
        if m % t == 0:
            best = t
    assert best is not None, (m, target)
    return best


def _params(*sem):
    return pltpu.CompilerParams(dimension_semantics=sem, vmem_limit_bytes=VMEM_LIMIT)


def _resident(block_shape, index_map):
    return pl.BlockSpec(block_shape, index_map, pipeline_mode=pl.Buffered(1))


def _mod_spec(layer, sel_fn, k):
    return lambda D: pl.BlockSpec((1, 1, D), lambda i: ((layer * 2 + sel_fn(i)) * N_MOD + k, 0, 0))


def _ada_kernel(a_ref, w_ref, o_ref):
    o_ref[0] = jnp.dot(a_ref[...], w_ref[0].astype(BF16), preferred_element_type=F32)


def ada_all(silu_pad, w_ada):
    L, D, N6 = w_ada.shape
    tn = _pick_tile(N6, 512, LANES)
    return pl.pallas_call(
        _ada_kernel,
        grid=(L, N6 // tn),
        in_specs=[pl.BlockSpec((16, D), lambda l, j: (0, 0)),
                  pl.BlockSpec((1, D, tn), lambda l, j: (l, 0, j))],
        out_specs=pl.BlockSpec((1, 16, tn), lambda l, j: (l, 0, j)),
        out_shape=jax.ShapeDtypeStruct((L, 16, N6), F32),
        compiler_params=_params("parallel", "parallel"),
        name="ada",
    )(silu_pad, w_ada)


def _norm_mm_kernel(x_ref, g_ref, sc_ref, sh_ref, w_ref, o_ref):
    x = x_ref[...]
    ms = jnp.mean(x * x, axis=-1, keepdims=True)
    h = x * lax.rsqrt(ms + EPS) * g_ref[0] * (1.0 + sc_ref[0]) + sh_ref[0]
    o_ref[...] = jnp.dot(h.astype(BF16), w_ref[0], preferred_element_type=F32)


def _post_proj_kernel(p_ref, cos_ref, sin_ref, gqa_ref, gkva_ref, gq_ref, gkn_ref, gkpe_ref,
                      gdq_ref, gdk_ref, gnq_ref, gnk_ref, wuq_ref, wukv_ref,
                      qa_o, ka_o, va_o, qb_o, kb_o, vb_o, qc_o, kc_o, vc_o,
                      *, mla_scale, diff_scale, na_scale):
    tm = p_ref.shape[0]
    cos_t = cos_ref[...]
    sin_t = sin_ref[...]
    lane = lax.broadcasted_iota(jnp.int32, (tm, LANES), 1)
    lo = lane < ROT_DIM
    first_half = (lane & (ROT_DIM - 1)) < ROT_DIM // 2
    ones_col = jnp.where(lane == 0, 1.0, 0.0).astype(BF16)

    def rope(x):
        sw = jnp.where(first_half, pltpu.roll(x, LANES - ROT_DIM // 2, 1),
                       pltpu.roll(x, ROT_DIM // 2, 1))
        return x * cos_t + sw * sin_t

    def rinv(x, n):
        return lax.rsqrt(jnp.sum(x * x, axis=-1, keepdims=True) * (1.0 / n) + EPS)

    qa = p_ref[:, _P_QA:_P_KVA]
    qa_n = (qa * rinv(qa, MLA_Q_RANK) * gqa_ref[0]).astype(BF16)
    qf = jnp.dot(qa_n, wuq_ref[0], preferred_element_type=F32)
    g_q = gq_ref[0]
    for h in range(MLA_HEADS):
        c0 = h * MLA_HEAD_PAD
        blk = qf[:, c0:c0 + MLA_HEAD_PAD]
        r = rinv(blk, MLA_NOPE + MLA_ROPE) * mla_scale
        qa_o[:, c0:c0 + LANES] = (blk[:, :LANES] * r * g_q[:, :LANES]).astype(BF16)
        qa_o[:, c0 + LANES:c0 + 2 * LANES] = rope(blk[:, LANES:] * r * g_q[:, LANES:]).astype(BF16)

    kva = p_ref[:, _P_KVA:_P_KPE]
    kva_n = (kva * rinv(kva, MLA_KV_RANK) * gkva_ref[0]).astype(BF16)
    kv = jnp.dot(kva_n, wukv_ref[0], preferred_element_type=F32)
    kpe = p_ref[:, _P_KPE:_P_KPE + LANES]
    kpe_r = rope(kpe * rinv(kpe, MLA_ROPE) * gkpe_ref[0]).astype(BF16)
    for h in range(MLA_HEADS):
        c0 = h * MLA_HEAD_PAD
        kn = kv[:, c0:c0 + LANES]
        ka_o[:, c0:c0 + LANES] = (kn * rinv(kn, MLA_NOPE) * gkn_ref[0]).astype(BF16)
        ka_o[:, c0 + LANES:c0 + 2 * LANES] = kpe_r
        va_o[:, c0:c0 + LANES] = kv[:, c0 + LANES:c0 + 2 * LANES].astype(BF16)
        va_o[:, c0 + LANES:c0 + 2 * LANES] = ones_col

    def subhead_norm_rope(x, g_ref, scale):
        sq = x * x
        s_lo = jnp.sum(jnp.where(lo, sq, 0.0), axis=-1, keepdims=True)
        s_hi = jnp.sum(jnp.where(lo, 0.0, sq), axis=-1, keepdims=True)
        r = jnp.where(lo, lax.rsqrt(s_lo * (1.0 / DIFF_QK) + EPS),
                      lax.rsqrt(s_hi * (1.0 / DIFF_QK) + EPS))
        return (rope(x * r * g_ref[0]) * scale).astype(BF16)

    for h in range(DIFF_HEADS):
        c0 = h * LANES
        qb_o[:, c0:c0 + LANES] = subhead_norm_rope(p_ref[:, _P_DQ + c0:_P_DQ + c0 + LANES], gdq_ref, diff_scale)
        kb_o[:, c0:c0 + LANES] = subhead_norm_rope(p_ref[:, _P_DK + c0:_P_DK + c0 + LANES], gdk_ref, 1.0)
        vb_o[:, 2 * c0:2 * c0 + LANES] = p_ref[:, _P_DV + c0:_P_DV + c0 + LANES].astype(BF16)
        vb_o[:, 2 * c0 + LANES:2 * c0 + 2 * LANES] = ones_col

    for h in range(NA_HEADS):
        c0 = h * NA_DIM
        x = p_ref[:, _P_NQ + c0:_P_NQ + c0 + NA_DIM]
        qc_o[:, c0:c0 + NA_DIM] = (x * rinv(x, NA_DIM) * gnq_ref[0] * na_scale).astype(BF16)
        x = p_ref[:, _P_NK + c0:_P_NK + c0 + NA_DIM]
        kc_o[:, c0:c0 + NA_DIM] = (x * rinv(x, NA_DIM) * gnk_ref[0]).astype(BF16)
    vc_o[...] = p_ref[:, _P_NV:_P_END].astype(BF16)


def _in_proj_kernel(x_ref, g_ref, sc_ref, sh_ref, w_ref, *rest, **scales):
    p_scr = rest[-1]
    _norm_mm_kernel(x_ref, g_ref, sc_ref, sh_ref, w_ref, p_scr)
    _post_proj_kernel(p_scr, *rest[:-1], **scales)


def in_proj(xa, norm_g, mods, w_in_p, cos_t, sin_t, gains, w_uq_p, w_ukv_b, layer, n_lat):
    T, D = xa.shape
    P = w_in_p.shape[-1]
    tm = ROW_BLOCK
    nlb = n_lat // tm
    sel = lambda i: jnp.where(i >= nlb, 1, 0)
    row = lambda w: pl.BlockSpec((tm, w), lambda i: (i, 0))
    gain_specs = [pl.BlockSpec((1, 1, g.shape[-1]), lambda i: (layer, 0, 0)) for g in gains]
    widths = (MLA_HEADS * MLA_HEAD_PAD, MLA_HEADS * MLA_HEAD_PAD, MLA_HEADS * 2 * LANES,
              DIFF_HEADS * LANES, DIFF_HEADS * LANES, DIFF_HEADS * 2 * LANES,
              NA_HEADS * NA_DIM, NA_HEADS * NA_DIM, NA_HEADS * NA_DIM)
    kern = functools.partial(
        _in_proj_kernel,
        mla_scale=(MLA_NOPE + MLA_ROPE) ** -0.5 * LOG2E,
        diff_scale=DIFF_QK ** -0.5 * LOG2E,
        na_scale=NA_DIM ** -0.5 * LOG2E)
    return pl.pallas_call(
        kern,
        grid=(T // tm,),
        in_specs=[row(D),
                  pl.BlockSpec((1, 1, D), lambda i: (layer, 0, 0)),
                  _mod_spec(layer, sel, 1)(D),
                  _mod_spec(layer, sel, 0)(D),
                  _resident((1, D, P), lambda i: (layer, 0, 0)),
                  row(LANES), row(LANES)] + gain_specs + [
            _resident((1,) + w_uq_p.shape[1:], lambda i: (layer, 0, 0)),
            _resident((1,) + w_ukv_b.shape[1:], lambda i: (layer, 0, 0))],
        out_specs=[row(w) for w in widths],
        out_shape=[jax.ShapeDtypeStruct((T, w), BF16) for w in widths],
        scratch_shapes=[pltpu.VMEM((tm, P), F32)],
        compiler_params=_params("parallel"),
        name="w_in",
    )(xa, norm_g, mods, mods, w_in_p, cos_t, sin_t, *gains, w_uq_p, w_ukv_b)


def _flash_kernel(*refs, tk, n_chunks, diff, lam_scale):
    if diff:
        q_ref, k_ref, v_ref, lam_ref, g_ref, o_ref, q_scr, s_scr, acc_scr = refs
    else:
        q_ref, k_ref, v_ref, o_ref, q_scr, s_scr, acc_scr = refs
    n_sub, rows, _ = q_scr.shape
    tq = q_ref.shape[0] // n_sub
    for u in range(n_sub):
        q = q_ref[u * tq:(u + 1) * tq, :]
        if diff:
            lane = lax.broadcasted_iota(jnp.int32, q.shape, 1)
            zero = jnp.zeros_like(q)
            q_scr[u, :tq, :] = jnp.where(lane < DIFF_QK, q, zero)
            q_scr[u, tq:, :] = jnp.where(lane >= DIFF_QK, q, zero)
        else:
            q_scr[u] = q
    acc_scr[...] = jnp.zeros_like(acc_scr)

    def qk(u, c, slot):
        s_scr[u, slot] = lax.dot_general(q_scr[u], k_ref[c * tk:(c + 1) * tk, :], NT_DIMS,
                                         preferred_element_type=F32)

    def consume(u, c, slot, m):
        s = s_scr[u, slot]
        m_new = jnp.maximum(m, jnp.max(s, axis=-1, keepdims=True))
        alpha = jnp.exp2(m - m_new)
        p = jnp.exp2(s - m_new).astype(BF16)
        acc_scr[u] = alpha * acc_scr[u] + jnp.dot(
            p, v_ref[c * tk:(c + 1) * tk, :], preferred_element_type=F32)
        return m_new

    ms = [jnp.full((rows, 1), -jnp.inf, F32) for _ in range(n_sub)]
    for u in range(n_sub):
        qk(u, 0, 0)
    for c in range(n_chunks):
        for u in range(n_sub):
            if c + 1 < n_chunks:
                qk(u, c + 1, (c + 1) % 2)
            ms[u] = consume(u, c, c % 2, ms[u])
    for u in range(n_sub):
        acc = acc_scr[u]
        o = acc[:, :LANES] / acc[:, LANES:LANES + 1]
        if diff:
            o = o[:tq] - lam_ref[...] * o[tq:]
            msq = jnp.mean(o * o, axis=-1, keepdims=True)
            o = o * lax.rsqrt(msq + EPS) * g_ref[0] * lam_scale
        o_ref[u * tq:(u + 1) * tq, :] = o.astype(o_ref.dtype)


def flash(q, k, v, *, heads, dk, n_q, q_start, n_k, k_start, tq_target, tk_target,
          diff_args=None, n_sub=1, name="flash"):
    tq = _pick_tile(math.gcd(n_q, q_start) if q_start else n_q, tq_target)
    tk = _pick_tile(n_k, tk_target, LANES)
    assert k_start % n_k == 0
    q_blk0 = q_start // tq
    k_blk0 = k_start // n_k
    vw = 2 * LANES
    in_specs = [pl.BlockSpec((tq, dk), lambda h, i: (q_blk0 + i, h)),
                pl.BlockSpec((n_k, dk), lambda h, i: (k_blk0, h)),
                pl.BlockSpec((n_k, vw), lambda h, i: (k_blk0, h))]
    args = [q, k, v]
    diff = diff_args is not None
    lam_scale = 1.0
    if diff:
        lam_row, subln, layer, lam_scale = diff_args
        in_specs += [pl.BlockSpec((1, LANES), lambda h, i: (0, 0)),
                     pl.BlockSpec((1, 1, LANES), lambda h, i: (layer, 0, 0))]
        args += [lam_row, subln]
    rows = (2 if diff else 1) * tq // n_sub
    kern = functools.partial(_flash_kernel, tk=tk, n_chunks=n_k // tk, diff=diff,
                             lam_scale=lam_scale)
    return pl.pallas_call(
        kern,
        grid=(heads, n_q // tq),
        in_specs=in_specs,
        out_specs=pl.BlockSpec((tq, LANES), lambda h, i: (i, h)),
        out_shape=jax.ShapeDtypeStruct((n_q, heads * LANES), BF16),
        scratch_shapes=[pltpu.VMEM((n_sub, rows, dk), BF16),
                        pltpu.VMEM((n_sub, 2, rows, tk), F32),
                        pltpu.VMEM((n_sub, rows, vw), F32)],
        compiler_params=_params("parallel", "arbitrary"),
        name=name,
    )(*args)


def _with_ones_column(v3):
    T, H, d = v3.shape
    pad = jnp.zeros((T, H, 2 * LANES - d), BF16).at[:, :, 0].set(1.0)
    return jnp.concatenate([v3.astype(BF16), pad], axis=-1).reshape(T, H * 2 * LANES)


def _na_kernel(q_ref, k0_ref, k1_ref, k2_ref, v0_ref, v1_ref, v2_ref, kc_ref, vc_ref, bias_ref,
               o_ref):
    k_refs = (k0_ref, k1_ref, k2_ref, kc_ref)
    v_refs = (v0_ref, v1_ref, v2_ref, vc_ref)
    blk = k0_ref.shape[0]
    for h in range(NA_HEADS):
        cols = slice(h * NA_DIM, (h + 1) * NA_DIM)
        q = q_ref[:, cols]
        s_parts = []
        for b in range(4):
            s = lax.dot_general(q, k_refs[b][:, cols], NT_DIMS, preferred_element_type=F32)
            if b < 3:
                s = s + bias_ref[0, 0, h, :, b * blk:(b + 1) * blk]
            s_parts.append(s)
        m = s_parts[3].max(axis=-1, keepdims=True)
        for b in range(3):
            m = jnp.maximum(m, s_parts[b].max(axis=-1, keepdims=True))
        l = jnp.zeros_like(m)
        acc = jnp.zeros((q.shape[0], NA_DIM), F32)
        for b in range(4):
            p = jnp.exp2(s_parts[b] - m)
            l = l + p.sum(axis=-1, keepdims=True)
            acc = acc + jnp.dot(p.astype(BF16), v_refs[b][:, cols], preferred_element_type=F32)
        o_ref[:, cols] = (acc / l).astype(o_ref.dtype)


def _na_bias_tables(rpb, rows):
    R = NA_ROWS_PER_GROUP
    G = rows // R
    assert rows % R == 0 and G >= 3 and rows >= NA_KH and NA_KH == 2 * R
    n_rr, n_cr = 2 * NA_KH - 1, 2 * NA_KW - 1
    qc = np.arange(GRID_W)[:, None]
    kc = np.arange(GRID_W)[None, :]
    wc = np.clip(qc - NA_KW // 2, 0, GRID_W - NA_KW)
    valid_c = (kc >= wc) & (kc < wc + NA_KW)
    col_rel = np.clip(kc - qc + NA_KW - 1, 0, n_cr - 1).reshape(-1)
    c_sel = (np.arange(n_cr)[:, None] == col_rel[None, :]).astype(np.float32)
    tabs = []
    for g in (0, 1, G - 1):
        j = np.arange(R)[:, None, None]
        b = np.arange(3)[None, :, None]
        kr = np.arange(R)[None, None, :]
        r = R * g + j
        rs = np.clip(r - NA_KH // 2, 0, rows - NA_KH)
        keyrow = R * (g - 1 + b) + kr
        valid_r = np.broadcast_to((keyrow >= rs) & (keyrow < rs + NA_KH), (R, 3, R))
        row_rel = np.broadcast_to(np.clip(keyrow - r + NA_KH - 1, 0, n_rr - 1), (R, 3, R)).reshape(-1)
        r_sel = (row_rel[:, None] == np.arange(n_rr)[None, :]).astype(np.float32)
        t = jnp.einsum('ar,lhrc,cq->lhaq', r_sel, rpb.astype(F32), c_sel,
                       precision=lax.Precision.HIGHEST)
        L, H = t.shape[:2]
        t = t.reshape(L, H, R, 3, R, GRID_W, GRID_W).transpose(0, 1, 2, 5, 3, 4, 6)
        valid = valid_r[:, None, :, :, None] & valid_c[None, :, None, None, :]
        t = jnp.where(valid[None, None], t * LOG2E, NEG_BIG)
        tabs.append(t.reshape(L, H, R * GRID_W, 3 * R * GRID_W))
    return jnp.stack(tabs, axis=1)


def neighborhood(q, k, v, bias_tabs, n_lat, layer):
    blk = NA_ROWS_PER_GROUP * GRID_W
    G = n_lat // blk
    n_ctx = q.shape[0] - n_lat
    assert n_ctx == blk, "context length must equal one key block"
    W = NA_HEADS * NA_DIM

    def kv_spec(off):
        return pl.BlockSpec((blk, W), lambda g: (jnp.clip(g + off, 0, G - 1), 0))

    ctx_spec = pl.BlockSpec((blk, W), lambda g: (G, 0))
    bias_spec = pl.BlockSpec(
        (1, 1, NA_HEADS, blk, 3 * blk),
        lambda g: (layer, jnp.where(g == 0, 0, jnp.where(g == G - 1, 2, 1)), 0, 0, 0))
    return pl.pallas_call(
        _na_kernel,
        grid=(G,),
        in_specs=[pl.BlockSpec((blk, W), lambda g: (g, 0)),
                  kv_spec(-1), kv_spec(0), kv_spec(1),
                  kv_spec(-1), kv_spec(0), kv_spec(1),
                  ctx_spec, ctx_spec, bias_spec],
        out_specs=pl.BlockSpec((blk, W), lambda g: (g, 0)),
        out_shape=jax.ShapeDtypeStruct((n_lat, W), BF16),
        compiler_params=_params("parallel"),
        name="na",
    )(q, k, k, k, v, v, v, k, v, bias_tabs)


def _out_proj_kernel(oa_ref, ob_ref, oc_ref, oa2_ref, ob2_ref, oc2_ref, x_ref, w_ref, g1_ref,
                     gn_ref, sc_ref, sh_ref, rw_ref, rb_ref, x1_o, h2_o, lg_o, *, n_lat_blocks):
    is_ctx = pl.program_id(0) >= n_lat_blocks
    wa = oa_ref.shape[1]
    wb = ob_ref.shape[1]
    oa = jnp.where(is_ctx, oa2_ref[...], oa_ref[...])
    ob = jnp.where(is_ctx, ob2_ref[...], ob_ref[...])
    oc = jnp.where(is_ctx, oc2_ref[...], oc_ref[...])
    o = jnp.dot(oa, w_ref[0, 0:wa, :], preferred_element_type=F32)
    o = o + jnp.dot(ob, w_ref[0, wa:wa + wb, :], preferred_element_type=F32)
    o = o + jnp.dot(oc, w_ref[0, wa + wb:, :], preferred_element_type=F32)
    x1 = x_ref[...] + g1_ref[0] * o
    x1_o[...] = x1
    ms = jnp.mean(x1 * x1, axis=-1, keepdims=True)
    h2 = x1 * lax.rsqrt(ms + EPS) * gn_ref[0] * (1.0 + sc_ref[0]) + sh_ref[0]
    h2_o[...] = h2
    h2_hi = h2.astype(BF16)
    h2_lo = (h2 - h2_hi.astype(F32)).astype(BF16)
    a = (jnp.dot(h2_hi, rw_ref[0], preferred_element_type=F32)
         + jnp.dot(h2_lo, rw_ref[0], preferred_element_type=F32))
    lg_o[...] = a[:, :LANES] + a[:, LANES:] + rb_ref[0]


def out_proj(o_lat, o_ctx, xa, w_out_b, mods, ffn_g, rw, rb, layer):
    n_lat = o_lat[0].shape[0]
    M = n_lat + (o_ctx[0].shape[0] if o_ctx is not None else 0)
    if o_ctx is None:
        o_ctx = o_lat
    D = xa.shape[1]
    tm = ROW_BLOCK
    nlb = n_lat // tm
    lat = lambda a: pl.BlockSpec((tm, a.shape[1]), lambda i: (jnp.minimum(i, nlb - 1), 0))
    ctx = lambda a: pl.BlockSpec((tm, a.shape[1]), lambda i: (jnp.maximum(i - nlb, 0), 0))
    sel = lambda i: jnp.where(i >= nlb, 1, 0)
    return pl.pallas_call(
        functools.partial(_out_proj_kernel, n_lat_blocks=nlb),
        grid=(M // tm,),
        in_specs=[lat(o_lat[0]), lat(o_lat[1]), lat(o_lat[2]),
                  ctx(o_ctx[0]), ctx(o_ctx[1]), ctx(o_ctx[2]),
                  pl.BlockSpec((tm, D), lambda i: (i, 0)),
                  _resident((1,) + w_out_b.shape[1:], lambda i: (layer, 0, 0)),
                  _mod_spec(layer, sel, 2)(D),
                  pl.BlockSpec((1, 1, D), lambda i: (layer, 0, 0)),
                  _mod_spec(layer, sel, 4)(D),
                  _mod_spec(layer, sel, 3)(D),
                  _resident((1, D, 2 * LANES), lambda i: (layer, 0, 0)),
                  pl.BlockSpec((1, 1, LANES), lambda i: (layer, 0, 0))],
        out_specs=[pl.BlockSpec((tm, D), lambda i: (i, 0)),
                   pl.BlockSpec((tm, D), lambda i: (i, 0)),
                   pl.BlockSpec((tm, LANES), lambda i: (i, 0))],
        out_shape=[jax.ShapeDtypeStruct((M, D), F32),
                   jax.ShapeDtypeStruct((M, D), F32),
                   jax.ShapeDtypeStruct((M, LANES), F32)],
        compiler_params=_params("parallel"),
        name="w_out",
    )(*o_lat, *o_ctx, xa, w_out_b, mods, ffn_g, mods, mods, rw, rb)


def _deinterleave_matrix():
    p = np.zeros((MXU_DIM, MXU_DIM), np.float32)
    i = np.arange(MXU_DIM // 2)
    p[2 * i, i] = 1.0
    p[2 * i + 1, MXU_DIM // 2 + i] = 1.0
    return jnp.asarray(p, BF16)


def _deint_kernel(w_ref, p_ref, o_ref):
    for c in range(w_ref.shape[-1] // MXU_DIM):
        cols = slice(c * MXU_DIM, (c + 1) * MXU_DIM)
        w = w_ref[0, 0, :, cols].astype(BF16)
        o_ref[0, :, cols] = jnp.dot(w, p_ref[...], preferred_element_type=F32).astype(BF16)


def deinterleave_gu(w_gu, layer):
    _, E, D, F2 = w_gu.shape
    tr = _pick_tile(D, 2 * ROW_BLOCK)
    return pl.pallas_call(
        _deint_kernel,
        grid=(E, D // tr),
        in_specs=[pl.BlockSpec((1, 1, tr, F2), lambda e, j: (layer, e, j, 0)),
                  pl.BlockSpec((MXU_DIM, MXU_DIM), lambda e, j: (0, 0))],
        out_specs=pl.BlockSpec((1, tr, F2), lambda e, j: (e, j, 0)),
        out_shape=jax.ShapeDtypeStruct((E, D, F2), BF16),
        compiler_params=_params("parallel", "parallel"),
        name="deint_gu",
    )(w_gu, _deinterleave_matrix())


def _moe_kernel(blk_e_ref, n_used_ref, x_ref, wgu_ref, bgu_ref, wd_ref, bd_ref, o_ref, wd_scr):
    i = pl.program_id(0)
    e = blk_e_ref[i]
    e_prev = blk_e_ref[jnp.maximum(i - 1, 0)]

    @pl.when((i == 0) | (e != e_prev))
    def _():
        wd_scr[...] = wd_ref[0, 0].astype(BF16)

    @pl.when(i < n_used_ref[0])
    def _():
        x = x_ref[...].astype(BF16)
        gu = jnp.dot(x, wgu_ref[0], preferred_element_type=F32) + bgu_ref[0]
        acts = []
        for c in range(gu.shape[1] // MXU_DIM):
            g = gu[:, c * MXU_DIM:c * MXU_DIM + LANES]
            u = gu[:, c * MXU_DIM + LANES:(c + 1) * MXU_DIM]
            glu = jnp.minimum(g, SWIGLU_LIMIT)
            lin = jnp.clip(u, -SWIGLU_LIMIT, SWIGLU_LIMIT)
            acts.append((glu * jax.nn.sigmoid(SWIGLU_ALPHA * glu) * (lin + 1.0)).astype(BF16))
        act = jnp.concatenate(acts, axis=1)
        o_ref[...] = jnp.dot(act, wd_scr[...], preferred_element_type=F32) + bd_ref[0, 0]

    @pl.when(i >= n_used_ref[0])
    def _():
        o_ref[...] = jnp.zeros_like(o_ref)


def moe_experts(xb, blk_e, n_used, wgu, bgu, w_down, b_down, layer):
    cap, D = xb.shape
    n_blocks = cap // MOE_BLOCK
    F2 = wgu.shape[-1]
    F = w_down.shape[2]
    grid_spec = pltpu.PrefetchScalarGridSpec(
        num_scalar_prefetch=2,
        grid=(n_blocks,),
        in_specs=[pl.BlockSpec((MOE_BLOCK, D), lambda i, e, n: (jnp.minimum(i, n[0] - 1), 0)),
                  pl.BlockSpec((1, D, F2), lambda i, e, n: (e[i], 0, 0)),
                  pl.BlockSpec((1, 1, F2), lambda i, e, n: (e[i], 0, 0)),
                  pl.BlockSpec((1, 1, F, D), lambda i, e, n: (layer, e[i], 0, 0)),
                  pl.BlockSpec((1, 1, 1, D), lambda i, e, n: (layer, e[i], 0, 0))],
        out_specs=pl.BlockSpec((MOE_BLOCK, D), lambda i, e, n: (i, 0)),
        scratch_shapes=[pltpu.VMEM((F, D), BF16)],
    )
    return pl.pallas_call(
        _moe_kernel,
        grid_spec=grid_spec,
        out_shape=jax.ShapeDtypeStruct((cap, D), F32),
        compiler_params=_params("arbitrary"),
        name="moe_experts",
    )(blk_e, n_used, xb, wgu, bgu, w_down, b_down)


def _combine_kernel(pc0, pc1, pc2, pc3, pn0, pn1, pn2, pn3, g_ref, x_ref, g2_ref, yb_hbm,
                    o_ref, buf, sem):
    i = pl.program_id(0)
    nb = pl.num_programs(0)
    tb = o_ref.shape[0]
    slot = i % 2

    def issue(pos_refs, s):
        for k in range(TOP_K):
            def body(t, carry, k=k):
                r = pos_refs[k][t]
                pltpu.make_async_copy(yb_hbm.at[pl.ds(r, 1), :],
                                      buf.at[s, pl.ds(k * tb + t, 1), :], sem.at[s]).start()
                return carry
            lax.fori_loop(0, tb, body, 0, unroll=8)

    @pl.when(i == 0)
    def _():
        issue((pc0, pc1, pc2, pc3), 0)

    @pl.when(i + 1 < nb)
    def _():
        issue((pn0, pn1, pn2, pn3), 1 - slot)

    pltpu.make_async_copy(yb_hbm.at[pl.ds(0, TOP_K * tb), :], buf.at[slot], sem.at[slot]).wait()
    y = jnp.zeros(o_ref.shape, F32)
    for k in range(TOP_K):
        y = y + g_ref[:, k:k + 1] * buf[slot, k * tb:(k + 1) * tb, :]
    o_ref[...] = x_ref[...] + g2_ref[0] * y


def moe_combine(yb, pos, gates, x1, mods, layer, n_lat):
    n, D = x1.shape
    tb = COMBINE_BLOCK
    nb = n // tb
    nlb = n_lat // tb
    sel = lambda i: jnp.where(i >= nlb, 1, 0)
    cur = pl.BlockSpec((tb,), lambda i: (i,), memory_space=pltpu.SMEM)
    nxt = pl.BlockSpec((tb,), lambda i: (jnp.minimum(i + 1, nb - 1),), memory_space=pltpu.SMEM)
    return pl.pallas_call(
        _combine_kernel,
        grid=(nb,),
        in_specs=[cur] * TOP_K + [nxt] * TOP_K + [
            pl.BlockSpec((tb, TOP_K), lambda i: (i, 0)),
            pl.BlockSpec((tb, D), lambda i: (i, 0)),
            _mod_spec(layer, sel, 5)(D),
            pl.BlockSpec(memory_space=pl.ANY)],
        out_specs=pl.BlockSpec((tb, D), lambda i: (i, 0)),
        out_shape=jax.ShapeDtypeStruct((n, D), F32),
        scratch_shapes=[pltpu.VMEM((2, TOP_K * tb, D), F32),
                        pltpu.SemaphoreType.DMA((2,))],
        compiler_params=_params("arbitrary"),
        name="moe_combine",
    )(pos[0], pos[1], pos[2], pos[3], pos[0], pos[1], pos[2], pos[3], gates, x1, mods, yb)


def _route_kernel(lg_ref, e_o, g_o, r_o, cnt_o, run_scr):
    i = pl.program_id(0)

    @pl.when(i == 0)
    def _():
        run_scr[...] = jnp.zeros_like(run_scr)

    tb = lg_ref.shape[0]
    lane = lax.broadcasted_iota(jnp.int32, (tb, LANES), 1)
    x = jnp.where(lane < N_EXPERTS, lg_ref[...], -jnp.inf)
    vals, hots = [], []
    for k in range(TOP_K):
        m = jnp.max(x, axis=-1, keepdims=True)
        idx = jnp.min(jnp.where(x == m, lane, LANES), axis=-1, keepdims=True)
        hot = lane == idx
        e_o[:, k:k + 1] = idx
        vals.append(m)
        hots.append(hot)
        x = jnp.where(hot, -jnp.inf, x)
    exps = [jnp.exp(v - vals[0]) for v in vals]
    denom = exps[0] + exps[1] + exps[2] + exps[3]
    for k in range(TOP_K):
        g_o[:, k:k + 1] = exps[k] / denom
    picked = (hots[0] | hots[1] | hots[2] | hots[3]).astype(BF16)
    r_idx = lax.broadcasted_iota(jnp.int32, (tb, tb), 0)
    c_idx = lax.broadcasted_iota(jnp.int32, (tb, tb), 1)
    before = (c_idx < r_idx).astype(BF16)
    seen = jnp.dot(before, picked, preferred_element_type=F32) + run_scr[...]
    for k in range(TOP_K):
        rank = jnp.sum(jnp.where(hots[k], seen, 0.0), axis=-1, keepdims=True)
        r_o[:, k:k + 1] = rank.astype(jnp.int32)
    run_scr[...] = run_scr[...] + jnp.sum(picked.astype(F32), axis=0, keepdims=True)
    cnt_o[...] = run_scr[...]


def route(logits):
    n = logits.shape[0]
    tb = ROW_BLOCK
    out = lambda: pl.BlockSpec((tb, TOP_K), lambda i: (i, 0))
    return pl.pallas_call(
        _route_kernel,
        grid=(n // tb,),
        in_specs=[pl.BlockSpec((tb, LANES), lambda i: (i, 0))],
        out_specs=[out(), out(), out(), pl.BlockSpec((1, LANES), lambda i: (0, 0))],
        out_shape=[jax.ShapeDtypeStruct((n, TOP_K), jnp.int32),
                   jax.ShapeDtypeStruct((n, TOP_K), F32),
                   jax.ShapeDtypeStruct((n, TOP_K), jnp.int32),
                   jax.ShapeDtypeStruct((1, LANES), F32)],
        scratch_shapes=[pltpu.VMEM((1, LANES), F32)],
        compiler_params=_params("arbitrary"),
        name="route",
    )(logits)


def _dispatch_kernel(pe_ref, cnt_ref, p0, p1, p2, p3, x_ref, xb_hbm, src, zeros, sem, zsem):
    i = pl.program_id(0)
    nb = pl.num_programs(0)
    tb = x_ref.shape[0]
    slot = i % 2
    pos_refs = (p0, p1, p2, p3)

    def zero_copy(row0):
        return pltpu.make_async_copy(zeros, xb_hbm.at[pl.ds(row0, MOE_BLOCK), :], zsem)

    @pl.when(i == 0)
    def _():
        zeros[...] = jnp.zeros_like(zeros)
        for e in range(N_EXPERTS):
            @pl.when(cnt_ref[e] > 0)
            def _(e=e):
                zero_copy(pl.multiple_of(pe_ref[e] - MOE_BLOCK, MOE_BLOCK)).start()
        n_used = pe_ref[N_EXPERTS - 1] // MOE_BLOCK
        n_blocks = xb_hbm.shape[0] // MOE_BLOCK

        def fill(b, carry):
            zero_copy(pl.multiple_of(b * MOE_BLOCK, MOE_BLOCK)).start()
            return carry

        def drain(b, carry):
            zero_copy(0).wait()
            return carry

        lax.fori_loop(n_used, n_blocks, fill, 0)
        lax.fori_loop(n_used, n_blocks, drain, 0)
        for e in range(N_EXPERTS):
            @pl.when(cnt_ref[e] > 0)
            def _():
                zero_copy(0).wait()

    def wait_slot(s):
        for _ in range(TOP_K):
            pltpu.make_async_copy(src.at[s], xb_hbm.at[pl.ds(0, tb), :], sem.at[s]).wait()

    @pl.when(i >= 2)
    def _():
        wait_slot(slot)

    src[slot] = x_ref[...]
    for k in range(TOP_K):
        def body(t, carry, k=k):
            r = pos_refs[k][t]
            pltpu.make_async_copy(src.at[slot, pl.ds(t, 1), :], xb_hbm.at[pl.ds(r, 1), :],
                                  sem.at[slot]).start()
            return carry
        lax.fori_loop(0, tb, body, 0, unroll=8)

    @pl.when(i == nb - 1)
    def _():
        wait_slot(slot)

    @pl.when((i == nb - 1) & (i >= 1))
    def _():
        wait_slot(1 - slot)


def moe_dispatch(h2, pos, pad_end, counts, cap):
    n, D = h2.shape
    tb = COMBINE_BLOCK
    grid_spec = pltpu.PrefetchScalarGridSpec(
        num_scalar_prefetch=2,
        grid=(n // tb,),
        in_specs=[pl.BlockSpec((tb,), lambda i, pe, c: (i,), memory_space=pltpu.SMEM)] * TOP_K + [
            pl.BlockSpec((tb, D), lambda i, pe, c: (i, 0))],
        out_specs=pl.BlockSpec(memory_space=pl.ANY),
        scratch_shapes=[pltpu.VMEM((2, tb, D), F32),
                        pltpu.VMEM((MOE_BLOCK, D), F32),
                        pltpu.SemaphoreType.DMA((2,)),
                        pltpu.SemaphoreType.DMA(())],
    )
    return pl.pallas_call(
        _dispatch_kernel,
        grid_spec=grid_spec,
        out_shape=jax.ShapeDtypeStruct((cap, D), F32),
        compiler_params=_params("arbitrary"),
        name="moe_dispatch",
    )(pad_end, counts, pos[0], pos[1], pos[2], pos[3], h2)


def moe_ffn(x1, h2, logits, wgu, bgu, w_down, b_down, mods, layer, n_lat):
    n, D = h2.shape
    top_e, gates, rank, cnt = route(logits)
    n_assign = n * TOP_K
    counts = cnt[0, :N_EXPERTS].astype(jnp.int32)
    padded = (counts + MOE_BLOCK - 1) // MOE_BLOCK * MOE_BLOCK
    pad_end = jnp.cumsum(padded)
    pad_start = pad_end - padded
    hot = top_e[:, :, None] == jnp.arange(N_EXPERTS, dtype=jnp.int32)[None, None, :]
    pos = (rank + jnp.sum(jnp.where(hot, pad_start[None, None, :], 0), axis=-1)).T
    n_blocks = -(-n_assign // MOE_BLOCK) + N_EXPERTS
    cap = n_blocks * MOE_BLOCK
    blk_row0 = jnp.arange(n_blocks, dtype=jnp.int32) * MOE_BLOCK
    blk_e = jnp.minimum(jnp.sum((pad_end[None, :] <= blk_row0[:, None]).astype(jnp.int32), axis=1),
                        N_EXPERTS - 1)
    n_used = pad_end[-1:] // MOE_BLOCK
    xb = moe_dispatch(h2, pos, pad_end, counts, cap)
    yb = moe_experts(xb, blk_e, n_used, wgu, bgu, w_down, b_down, layer)
    return moe_combine(yb, pos, gates, x1, mods, layer, n_lat)


def _rope_tables(n_lat, n_ctx):
    quarter = ROT_DIM // 4
    inv = 1.0 / (ROPE_THETA ** (jnp.arange(quarter, dtype=F32) / quarter))
    t = jnp.arange(n_lat)
    row = (t // GRID_W).astype(F32)
    col = (t % GRID_W).astype(F32)
    ang = jnp.concatenate([row[:, None] * inv, col[:, None] * inv], axis=-1)
    ang = jnp.concatenate([ang, jnp.zeros((n_ctx, ROT_DIM // 2), F32)], axis=0)
    cos, sin = jnp.cos(ang), jnp.sin(ang)
    reps = LANES // ROT_DIM
    cos_t = jnp.tile(jnp.concatenate([cos, cos], axis=-1), (1, reps))
    sin_t = jnp.tile(jnp.concatenate([-sin, sin], axis=-1), (1, reps))
    return cos_t, sin_t


def _pad_w_in(w):
    L, D, _ = w.shape
    a = MLA_Q_RANK + MLA_KV_RANK + MLA_ROPE
    return jnp.concatenate([w[..., :a], jnp.zeros((L, D, _P_DQ - a), w.dtype), w[..., a:]], axis=-1)


def _pad_w_uq(w):
    L, r, _ = w.shape
    w4 = w.reshape(L, r, MLA_HEADS, MLA_NOPE + MLA_ROPE)
    w4 = jnp.concatenate(
        [w4, jnp.zeros((L, r, MLA_HEADS, MLA_HEAD_PAD - MLA_NOPE - MLA_ROPE), w.dtype)], axis=-1)
    return w4.reshape(L, r, MLA_HEADS * MLA_HEAD_PAD)


def _row3(a, width=None, tile=1):
    a = jnp.tile(a.astype(F32), (1, tile))
    if width is not None and width > a.shape[1]:
        a = jnp.concatenate([a, jnp.zeros((a.shape[0], width - a.shape[1]), F32)], axis=1)
    return a[:, None, :]


def kernel(x, c, ctx, c_ctx, w_ada, b_ada, attn_norm, ffn_norm, w_in, mla_qa_norm, w_uq,
           mla_kva_norm, w_ukv, mla_q_gain, mla_knope_gain, mla_kpe_gain, diff_q_gain, diff_k_gain,
           diff_lambda, diff_subln, na_q_gain, na_k_gain, na_rpb, w_out, router_w, router_b,
           w_gu, b_gu, w_down, b_down):
    depth = w_in.shape[0]
    B, N, D = x.shape
    assert B == 1
    C = ctx.shape[1]
    T = N + C
    assert N % ROW_BLOCK == 0 and C % ROW_BLOCK == 0
    rows = N // GRID_W
    cos_t, sin_t = _rope_tables(N, C)
    silu = jnp.stack([jax.nn.silu(c[0]), jax.nn.silu(c_ctx)])
    silu_pad = jnp.concatenate([silu, jnp.zeros((14, D), F32)], axis=0).astype(BF16)
    n_groups = w_gu.shape[-1] // MXU_DIM

    mods = ada_all(silu_pad, w_ada)[:, :2] + b_ada[:, None, :]
    mods = mods.reshape(depth * 2 * N_MOD, 1, D)

    w_in_p = _pad_w_in(w_in).astype(BF16)
    w_uq_p = _pad_w_uq(w_uq).astype(BF16)
    w_ukv_b = w_ukv.astype(BF16)
    w_out_b = w_out.astype(BF16)
    gains = (_row3(mla_qa_norm), _row3(mla_kva_norm), _row3(mla_q_gain, MLA_HEAD_PAD),
             _row3(mla_knope_gain), _row3(mla_kpe_gain, LANES),
             _row3(diff_q_gain, tile=2), _row3(diff_k_gain, tile=2),
             _row3(na_q_gain), _row3(na_k_gain))
    attn_g = _row3(attn_norm)
    ffn_g = _row3(ffn_norm)
    subln = _row3(diff_subln)
    rw = jnp.concatenate([router_w.astype(F32),
                          jnp.zeros((depth, D, LANES - N_EXPERTS), F32)], axis=-1)
    rw_hi = rw.astype(BF16)
    rw = jnp.concatenate([rw_hi, (rw - rw_hi.astype(F32)).astype(BF16)], axis=-1)
    rb = _row3(router_b, LANES)
    bgu_all = b_gu.astype(F32).reshape(depth, N_EXPERTS, n_groups, LANES, 2)
    bgu_all = bgu_all.transpose(0, 1, 2, 4, 3).reshape(depth, N_EXPERTS, 1, n_groups * MXU_DIM)
    bd_all = b_down.astype(F32)[:, :, None, :]
    na_bias = _na_bias_tables(na_rpb, rows)

    xa = jnp.concatenate([x[0], ctx[0]], axis=0)
    for l in range(depth):
        with_ctx_out = l < depth - 1
        lam_init = 0.8 - 0.6 * math.exp(-0.3 * l)
        q_a, k_a, v_a, q_b, k_b, v_b, q_c, k_c, v_c = in_proj(
            xa, attn_g, mods, w_in_p, cos_t, sin_t, gains, w_uq_p, w_ukv_b, l, N)

        lf = diff_lambda[l].astype(F32)
        lam = jnp.exp(jnp.sum(lf[0] * lf[1])) - jnp.exp(jnp.sum(lf[2] * lf[3])) + lam_init
        diff_args = (jnp.full((1, LANES), lam, F32), subln, l, 1.0 - lam_init)

        o_a = flash(q_a, k_a, v_a, heads=MLA_HEADS, dk=MLA_HEAD_PAD,
                    n_q=N, q_start=0, n_k=T, k_start=0, tq_target=512, tk_target=1280, name="mla")
        o_b = flash(q_b, k_b, v_b, heads=DIFF_HEADS, dk=2 * DIFF_QK,
                    n_q=N, q_start=0, n_k=T, k_start=0, tq_target=256, tk_target=1280,
                    diff_args=diff_args, name="diff")
        o_c = neighborhood(q_c, k_c, v_c, na_bias, N, l)
        o_ctx = None
        wgu = deinterleave_gu(w_gu, l)
        if with_ctx_out:
            v_cc = _with_ones_column(v_c[N:].reshape(C, NA_HEADS, NA_DIM))
            oc_a = flash(q_a, k_a, v_a, heads=MLA_HEADS, dk=MLA_HEAD_PAD,
                         n_q=C, q_start=N, n_k=C, k_start=N, tq_target=256, tk_target=256, name="mla_ctx")
            oc_b = flash(q_b, k_b, v_b, heads=DIFF_HEADS, dk=2 * DIFF_QK,
                         n_q=C, q_start=N, n_k=C, k_start=N, tq_target=256, tk_target=256,
                         diff_args=diff_args, name="diff_ctx")
            oc_c = flash(q_c, k_c[N:], v_cc, heads=NA_HEADS, dk=NA_DIM,
                         n_q=C, q_start=N, n_k=C, k_start=0, tq_target=256, tk_target=256, name="na_ctx")
            o_ctx = (oc_a, oc_b, oc_c)
        x1, h2, logits = out_proj((o_a, o_b, o_c), o_ctx, xa, w_out_b, mods, ffn_g, rw, rb, l)
        xa = moe_ffn(x1, h2, logits, wgu, bgu_all[l], w_down, bd_all, mods, l, N)
    return xa[:N][None]
```

```python
import functools
import math

import numpy as np
import jax
import jax.numpy as jnp
from jax import lax
from jax.experimental import pallas as pl
from jax.experimental.pallas import tpu as pltpu

F32 = jnp.float32
BF16 = jnp.bfloat16

GRID_W = 64
EPS = 1e-6
ROPE_THETA = 10000.0
ROT_DIM = 64
MLA_HEADS = 8
MLA_NOPE = 128
MLA_ROPE = 64
MLA_V = 128
MLA_Q_RANK = 512
MLA_KV_RANK = 256
MLA_HEAD_PAD = 256
DIFF_HEADS = 4
DIFF_QK = 64
DIFF_V = 128
NA_HEADS = 4
NA_DIM = 128
NA_KH = 8
NA_KW = 16
NA_ROWS_PER_GROUP = 4
N_EXPERTS = 32
TOP_K = 4
D_EXPERT = 768
SWIGLU_ALPHA = 1.702
SWIGLU_LIMIT = 7.0
MOE_BLOCK = 256
COMBINE_BLOCK = 128
ROW_BLOCK = 256
N_MOD = 6
NEG_BIG = -1e30
LOG2E = math.log2(math.e)

LANES = 128
MXU_DIM = 256
VMEM_LIMIT = 56 * 1024 * 1024

NT_DIMS = (((1,), (1,)), ((), ()))

_P_QA, _P_KVA, _P_KPE, _P_DQ, _P_DK, _P_DV, _P_NQ, _P_NK, _P_NV, _P_END = (
    0, 512, 768, 1024, 1536, 2048, 2560, 3072, 3584, 4096)


def _pick_tile(m, target, mult=8):
    best = None
    for t in range(mult, min(m, target) + 1, mult):
        if m % t == 0:
            best = t
    assert best is not None, (m, target)
    return best


def _params(*sem):
    return pltpu.CompilerParams(dimension_semantics=sem, vmem_limit_bytes=VMEM_LIMIT)


def _resident(block_shape, index_map):
    return pl.BlockSpec(block_shape, index_map, pipeline_mode=pl.Buffered(1))


def _mod_spec(layer, sel_fn, k):
    return lambda D: pl.BlockSpec((1, 1, D), lambda i: ((layer * 2 + sel_fn(i)) * N_MOD + k, 0, 0))


def _ada_kernel(a_ref, w_ref, o_ref):
    o_ref[0] = jnp.dot(a_ref[...], w_ref[0].astype(BF16), preferred_element_type=F32)


def ada_all(silu_pad, w_ada):
    L, D, N6 = w_ada.shape
    tn = _pick_tile(N6, 512, LANES)
    return pl.pallas_call(
        _ada_kernel,
        grid=(L, N6 // tn),
        in_specs=[pl.BlockSpec((16, D), lambda l, j: (0, 0)),
                  pl.BlockSpec((1, D, tn), lambda l, j: (l, 0, j))],
        out_specs=pl.BlockSpec((1, 16, tn), lambda l, j: (l, 0, j)),
        out_shape=jax.ShapeDtypeStruct((L, 16, N6), F32),
        compiler_params=_params("parallel", "parallel"),
        name="ada",
    )(silu_pad, w_ada)


def _norm_mm_kernel(x_ref, g_ref, sc_ref, sh_ref, w_ref, o_ref):
    x = x_ref[...]
    ms = jnp.mean(x * x, axis=-1, keepdims=True)
    h = x * lax.rsqrt(ms + EPS) * g_ref[0] * (1.0 + sc_ref[0]) + sh_ref[0]
    o_ref[...] = jnp.dot(h.astype(BF16), w_ref[0], preferred_element_type=F32)


def _post_proj_kernel(p_ref, cos_ref, sin_ref, gqa_ref, gkva_ref, gq_ref, gkn_ref, gkpe_ref,
                      gdq_ref, gdk_ref, gnq_ref, gnk_ref, wuq_ref, wukv_ref,
                      qa_o, ka_o, va_o, qb_o, kb_o, vb_o, qc_o, kc_o, vc_o,
                      *, mla_scale, diff_scale, na_scale):
    tm = p_ref.shape[0]
    cos_t = cos_ref[...]
    sin_t = sin_ref[...]
    lane = lax.broadcasted_iota(jnp.int32, (tm, LANES), 1)
    lo = lane < ROT_DIM
    first_half = (lane & (ROT_DIM - 1)) < ROT_DIM // 2
    ones_col = jnp.where(lane == 0, 1.0, 0.0).astype(BF16)

    def rope(x):
        sw = jnp.where(first_half, pltpu.roll(x, LANES - ROT_DIM // 2, 1),
                       pltpu.roll(x, ROT_DIM // 2, 1))
        return x * cos_t + sw * sin_t

    def rinv(x, n):
        return lax.rsqrt(jnp.sum(x * x, axis=-1, keepdims=True) * (1.0 / n) + EPS)

    qa = p_ref[:, _P_QA:_P_KVA]
    qa_n = (qa * rinv(qa, MLA_Q_RANK) * gqa_ref[0]).astype(BF16)
    qf = jnp.dot(qa_n, wuq_ref[0], preferred_element_type=F32)
    g_q = gq_ref[0]
    for h in range(MLA_HEADS):
        c0 = h * MLA_HEAD_PAD
        blk = qf[:, c0:c0 + MLA_HEAD_PAD]
        r = rinv(blk, MLA_NOPE + MLA_ROPE) * mla_scale
        qa_o[:, c0:c0 + LANES] = (blk[:, :LANES] * r * g_q[:, :LANES]).astype(BF16)
        qa_o[:, c0 + LANES:c0 + 2 * LANES] = rope(blk[:, LANES:] * r * g_q[:, LANES:]).astype(BF16)

    kva = p_ref[:, _P_KVA:_P_KPE]
    kva_n = (kva * rinv(kva, MLA_KV_RANK) * gkva_ref[0]).astype(BF16)
    kv = jnp.dot(kva_n, wukv_ref[0], preferred_element_type=F32)
    kpe = p_ref[:, _P_KPE:_P_KPE + LANES]
    kpe_r = rope(kpe * rinv(kpe, MLA_ROPE) * gkpe_ref[0]).astype(BF16)
    for h in range(MLA_HEADS):
        c0 = h * MLA_HEAD_PAD
        kn = kv[:, c0:c0 + LANES]
        ka_o[:, c0:c0 + LANES] = (kn * rinv(kn, MLA_NOPE) * gkn_ref[0]).astype(BF16)
        ka_o[:, c0 + LANES:c0 + 2 * LANES] = kpe_r
        va_o[:, c0:c0 + LANES] = kv[:, c0 + LANES:c0 + 2 * LANES].astype(BF16)
        va_o[:, c0 + LANES:c0 + 2 * LANES] = ones_col

    def subhead_norm_rope(x, g_ref, scale):
        sq = x * x
        s_lo = jnp.sum(jnp.where(lo, sq, 0.0), axis=-1, keepdims=True)
        s_hi = jnp.sum(jnp.where(lo, 0.0, sq), axis=-1, keepdims=True)
        r = jnp.where(lo, lax.rsqrt(s_lo * (1.0 / DIFF_QK) + EPS),
                      lax.rsqrt(s_hi * (1.0 / DIFF_QK) + EPS))
        return (rope(x * r * g_ref[0]) * scale).astype(BF16)

    for h in range(DIFF_HEADS):
        c0 = h * LANES
        qb_o[:, c0:c0 + LANES] = subhead_norm_rope(p_ref[:, _P_DQ + c0:_P_DQ + c0 + LANES], gdq_ref, diff_scale)
        kb_o[:, c0:c0 + LANES] = subhead_norm_rope(p_ref[:, _P_DK + c0:_P_DK + c0 + LANES], gdk_ref, 1.0)
        vb_o[:, 2 * c0:2 * c0 + LANES] = p_ref[:, _P_DV + c0:_P_DV + c0 + LANES].astype(BF16)
        vb_o[:, 2 * c0 + LANES:2 * c0 + 2 * LANES] = ones_col

    for h in range(NA_HEADS):
        c0 = h * NA_DIM
        x = p_ref[:, _P_NQ + c0:_P_NQ + c0 + NA_DIM]
        qc_o[:, c0:c0 + NA_DIM] = (x * rinv(x, NA_DIM) * gnq_ref[0] * na_scale).astype(BF16)
        x = p_ref[:, _P_NK + c0:_P_NK + c0 + NA_DIM]
        kc_o[:, c0:c0 + NA_DIM] = (x * rinv(x, NA_DIM) * gnk_ref[0]).astype(BF16)
    vc_o[...] = p_ref[:, _P_NV:_P_END].astype(BF16)


def _in_proj_kernel(x_ref, g_ref, sc_ref, sh_ref, w_ref, *rest, **scales):
    p_scr = rest[-1]
    _norm_mm_kernel(x_ref, g_ref, sc_ref, sh_ref, w_ref, p_scr)
    _post_proj_kernel(p_scr, *rest[:-1], **scales)


def in_proj(xa, norm_g, mods, w_in_p, cos_t, sin_t, gains, w_uq_p, w_ukv_b, layer, n_lat):
    T, D = xa.shape
    P = w_in_p.shape[-1]
    tm = ROW_BLOCK
    nlb = n_lat // tm
    sel = lambda i: jnp.where(i >= nlb, 1, 0)
    row = lambda w: pl.BlockSpec((tm, w), lambda i: (i, 0))
    gain_specs = [pl.BlockSpec((1, 1, g.shape[-1]), lambda i: (layer, 0, 0)) for g in gains]
    widths = (MLA_HEADS * MLA_HEAD_PAD, MLA_HEADS * MLA_HEAD_PAD, MLA_HEADS * 2 * LANES,
              DIFF_HEADS * LANES, DIFF_HEADS * LANES, DIFF_HEADS * 2 * LANES,
              NA_HEADS * NA_DIM, NA_HEADS * NA_DIM, NA_HEADS * NA_DIM)
    kern = functools.partial(
        _in_proj_kernel,
        mla_scale=(MLA_NOPE + MLA_ROPE) ** -0.5 * LOG2E,
        diff_scale=DIFF_QK ** -0.5 * LOG2E,
        na_scale=NA_DIM ** -0.5 * LOG2E)
    return pl.pallas_call(
        kern,
        grid=(T // tm,),
        in_specs=[row(D),
                  pl.BlockSpec((1, 1, D), lambda i: (layer, 0, 0)),
                  _mod_spec(layer, sel, 1)(D),
                  _mod_spec(layer, sel, 0)(D),
                  _resident((1, D, P), lambda i: (layer, 0, 0)),
                  row(LANES), row(LANES)] + gain_specs + [
            _resident((1,) + w_uq_p.shape[1:], lambda i: (layer, 0, 0)),
            _resident((1,) + w_ukv_b.shape[1:], lambda i: (layer, 0, 0))],
        out_specs=[row(w) for w in widths],
        out_shape=[jax.ShapeDtypeStruct((T, w), BF16) for w in widths],
        scratch_shapes=[pltpu.VMEM((tm, P), F32)],
        compiler_params=_params("parallel"),
        name="w_in",
    )(xa, norm_g, mods, mods, w_in_p, cos_t, sin_t, *gains, w_uq_p, w_ukv_b)


def _flash_kernel(*refs, tk, n_chunks, diff, lam_scale):
    if diff:
        q_ref, k_ref, v_ref, lam_ref, g_ref, o_ref, q_scr, s_scr, acc_scr = refs
    else:
        q_ref, k_ref, v_ref, o_ref, q_scr, s_scr, acc_scr = refs
    n_sub, rows, _ = q_scr.shape
    tq = q_ref.shape[0] // n_sub
    for u in range(n_sub):
        q = q_ref[u * tq:(u + 1) * tq, :]
        if diff:
            lane = lax.broadcasted_iota(jnp.int32, q.shape, 1)
            zero = jnp.zeros_like(q)
            q_scr[u, :tq, :] = jnp.where(lane < DIFF_QK, q, zero)
            q_scr[u, tq:, :] = jnp.where(lane >= DIFF_QK, q, zero)
        else:
            q_scr[u] = q
    acc_scr[...] = jnp.zeros_like(acc_scr)

    def qk(u, c, slot):
        s_scr[u, slot] = lax.dot_general(q_scr[u], k_ref[c * tk:(c + 1) * tk, :], NT_DIMS,
                                         preferred_element_type=F32)

    def consume(u, c, slot, m):
        s = s_scr[u, slot]
        m_new = jnp.maximum(m, jnp.max(s, axis=-1, keepdims=True))
        alpha = jnp.exp2(m - m_new)
        p = jnp.exp2(s - m_new).astype(BF16)
        acc_scr[u] = alpha * acc_scr[u] + jnp.dot(
            p, v_ref[c * tk:(c + 1) * tk, :], preferred_element_type=F32)
        return m_new

    ms = [jnp.full((rows, 1), -jnp.inf, F32) for _ in range(n_sub)]
    for u in range(n_sub):
        qk(u, 0, 0)
    for c in range(n_chunks):
        for u in range(n_sub):
            if c + 1 < n_chunks:
                qk(u, c + 1, (c + 1) % 2)
            ms[u] = consume(u, c, c % 2, ms[u])
    for u in range(n_sub):
        acc = acc_scr[u]
        o = acc[:, :LANES] / acc[:, LANES:LANES + 1]
        if diff:
            o = o[:tq] - lam_ref[...] * o[tq:]
            msq = jnp.mean(o * o, axis=-1, keepdims=True)
            o = o * lax.rsqrt(msq + EPS) * g_ref[0] * lam_scale
        o_ref[u * tq:(u + 1) * tq, :] = o.astype(o_ref.dtype)


def flash(q, k, v, *, heads, dk, n_q, q_start, n_k, k_start, tq_target, tk_target,
          diff_args=None, n_sub=1, name="flash"):
    tq = _pick_tile(math.gcd(n_q, q_start) if q_start else n_q, tq_target)
    tk = _pick_tile(n_k, tk_target, LANES)
    assert k_start % n_k == 0
    q_blk0 = q_start // tq
    k_blk0 = k_start // n_k
    vw = 2 * LANES
    in_specs = [pl.BlockSpec((tq, dk), lambda h, i: (q_blk0 + i, h)),
                pl.BlockSpec((n_k, dk), lambda h, i: (k_blk0, h)),
                pl.BlockSpec((n_k, vw), lambda h, i: (k_blk0, h))]
    args = [q, k, v]
    diff = diff_args is not None
    lam_scale = 1.0
    if diff:
        lam_row, subln, layer, lam_scale = diff_args
        in_specs += [pl.BlockSpec((1, LANES), lambda h, i: (0, 0)),
                     pl.BlockSpec((1, 1, LANES), lambda h, i: (layer, 0, 0))]
        args += [lam_row, subln]
    rows = (2 if diff else 1) * tq // n_sub
    kern = functools.partial(_flash_kernel, tk=tk, n_chunks=n_k // tk, diff=diff,
                             lam_scale=lam_scale)
    return pl.pallas_call(
        kern,
        grid=(heads, n_q // tq),
        in_specs=in_specs,
        out_specs=pl.BlockSpec((tq, LANES), lambda h, i: (i, h)),
        out_shape=jax.ShapeDtypeStruct((n_q, heads * LANES), BF16),
        scratch_shapes=[pltpu.VMEM((n_sub, rows, dk), BF16),
                        pltpu.VMEM((n_sub, 2, rows, tk), F32),
                        pltpu.VMEM((n_sub, rows, vw), F32)],
        compiler_params=_params("parallel", "arbitrary"),
        name=name,
    )(*args)


def _with_ones_column(v3):
    T, H, d = v3.shape
    pad = jnp.zeros((T, H, 2 * LANES - d), BF16).at[:, :, 0].set(1.0)
    return jnp.concatenate([v3.astype(BF16), pad], axis=-1).reshape(T, H * 2 * LANES)


def _na_kernel(q_ref, k0_ref, k1_ref, k2_ref, v0_ref, v1_ref, v2_ref, kc_ref, vc_ref, bias_ref,
               o_ref):
    k_refs = (k0_ref, k1_ref, k2_ref, kc_ref)
    v_refs = (v0_ref, v1_ref, v2_ref, vc_ref)
    blk = k0_ref.shape[0]
    for h in range(NA_HEADS):
        cols = slice(h * NA_DIM, (h + 1) * NA_DIM)
        q = q_ref[:, cols]
        s_parts = []
        for b in range(4):
            s = lax.dot_general(q, k_refs[b][:, cols], NT_DIMS, preferred_element_type=F32)
            if b < 3:
                s = s + bias_ref[0, 0, h, :, b * blk:(b + 1) * blk]
            s_parts.append(s)
        m = s_parts[3].max(axis=-1, keepdims=True)
        for b in range(3):
            m = jnp.maximum(m, s_parts[b].max(axis=-1, keepdims=True))
        l = jnp.zeros_like(m)
        acc = jnp.zeros((q.shape[0], NA_DIM), F32)
        for b in range(4):
            p = jnp.exp2(s_parts[b] - m)
            l = l + p.sum(axis=-1, keepdims=True)
            acc = acc + jnp.dot(p.astype(BF16), v_refs[b][:, cols], preferred_element_type=F32)
        o_ref[:, cols] = (acc / l).astype(o_ref.dtype)


def _na_bias_tables(rpb, rows):
    R = NA_ROWS_PER_GROUP
    G = rows // R
    assert rows % R == 0 and G >= 3 and rows >= NA_KH and NA_KH == 2 * R
    n_rr, n_cr = 2 * NA_KH - 1, 2 * NA_KW - 1
    qc = np.arange(GRID_W)[:, None]
    kc = np.arange(GRID_W)[None, :]
    wc = np.clip(qc - NA_KW // 2, 0, GRID_W - NA_KW)
    valid_c = (kc >= wc) & (kc < wc + NA_KW)
    col_rel = np.clip(kc - qc + NA_KW - 1, 0, n_cr - 1).reshape(-1)
    c_sel = (np.arange(n_cr)[:, None] == col_rel[None, :]).astype(np.float32)
    tabs = []
    for g in (0, 1, G - 1):
        j = np.arange(R)[:, None, None]
        b = np.arange(3)[None, :, None]
        kr = np.arange(R)[None, None, :]
        r = R * g + j
        rs = np.clip(r - NA_KH // 2, 0, rows - NA_KH)
        keyrow = R * (g - 1 + b) + kr
        valid_r = np.broadcast_to((keyrow >= rs) & (keyrow < rs + NA_KH), (R, 3, R))
        row_rel = np.broadcast_to(np.clip(keyrow - r + NA_KH - 1, 0, n_rr - 1), (R, 3, R)).reshape(-1)
        r_sel = (row_rel[:, None] == np.arange(n_rr)[None, :]).astype(np.float32)
        t = jnp.einsum('ar,lhrc,cq->lhaq', r_sel, rpb.astype(F32), c_sel,
                       precision=lax.Precision.HIGHEST)
        L, H = t.shape[:2]
        t = t.reshape(L, H, R, 3, R, GRID_W, GRID_W).transpose(0, 1, 2, 5, 3, 4, 6)
        valid = valid_r[:, None, :, :, None] & valid_c[None, :, None, None, :]
        t = jnp.where(valid[None, None], t * LOG2E, NEG_BIG)
        tabs.append(t.reshape(L, H, R * GRID_W, 3 * R * GRID_W))
    return jnp.stack(tabs, axis=1)


def neighborhood(q, k, v, bias_tabs, n_lat, layer):
    blk = NA_ROWS_PER_GROUP * GRID_W
    G = n_lat // blk
    n_ctx = q.shape[0] - n_lat
    assert n_ctx == blk, "context length must equal one key block"
    W = NA_HEADS * NA_DIM

    def kv_spec(off):
        return pl.BlockSpec((blk, W), lambda g: (jnp.clip(g + off, 0, G - 1), 0))

    ctx_spec = pl.BlockSpec((blk, W), lambda g: (G, 0))
    bias_spec = pl.BlockSpec(
        (1, 1, NA_HEADS, blk, 3 * blk),
        lambda g: (layer, jnp.where(g == 0, 0, jnp.where(g == G - 1, 2, 1)), 0, 0, 0))
    return pl.pallas_call(
        _na_kernel,
        grid=(G,),
        in_specs=[pl.BlockSpec((blk, W), lambda g: (g, 0)),
                  kv_spec(-1), kv_spec(0), kv_spec(1),
                  kv_spec(-1), kv_spec(0), kv_spec(1),
                  ctx_spec, ctx_spec, bias_spec],
        out_specs=pl.BlockSpec((blk, W), lambda g: (g, 0)),
        out_shape=jax.ShapeDtypeStruct((n_lat, W), BF16),
        compiler_params=_params("parallel"),
        name="na",
    )(q, k, k, k, v, v, v, k, v, bias_tabs)


def _out_proj_kernel(oa_ref, ob_ref, oc_ref, oa2_ref, ob2_ref, oc2_ref, x_ref, w_ref, g1_ref,
                     gn_ref, sc_ref, sh_ref, rw_ref, rb_ref, x1_o, h2_o, lg_o, *, n_lat_blocks):
    is_ctx = pl.program_id(0) >= n_lat_blocks
    wa = oa_ref.shape[1]
    wb = ob_ref.shape[1]
    oa = jnp.where(is_ctx, oa2_ref[...], oa_ref[...])
    ob = jnp.where(is_ctx, ob2_ref[...], ob_ref[...])
    oc = jnp.where(is_ctx, oc2_ref[...], oc_ref[...])
    o = jnp.dot(oa, w_ref[0, 0:wa, :], preferred_element_type=F32)
    o = o + jnp.dot(ob, w_ref[0, wa:wa + wb, :], preferred_element_type=F32)
    o = o + jnp.dot(oc, w_ref[0, wa + wb:, :], preferred_element_type=F32)
    x1 = x_ref[...] + g1_ref[0] * o
    x1_o[...] = x1
    ms = jnp.mean(x1 * x1, axis=-1, keepdims=True)
    h2 = x1 * lax.rsqrt(ms + EPS) * gn_ref[0] * (1.0 + sc_ref[0]) + sh_ref[0]
    h2_o[...] = h2
    h2_hi = h2.astype(BF16)
    h2_lo = (h2 - h2_hi.astype(F32)).astype(BF16)
    a = (jnp.dot(h2_hi, rw_ref[0], preferred_element_type=F32)
         + jnp.dot(h2_lo, rw_ref[0], preferred_element_type=F32))
    lg_o[...] = a[:, :LANES] + a[:, LANES:] + rb_ref[0]


def out_proj(o_lat, o_ctx, xa, w_out_b, mods, ffn_g, rw, rb, layer):
    n_lat = o_lat[0].shape[0]
    M = n_lat + (o_ctx[0].shape[0] if o_ctx is not None else 0)
    if o_ctx is None:
        o_ctx = o_lat
    D = xa.shape[1]
    tm = ROW_BLOCK
    nlb = n_lat // tm
    lat = lambda a: pl.BlockSpec((tm, a.shape[1]), lambda i: (jnp.minimum(i, nlb - 1), 0))
    ctx = lambda a: pl.BlockSpec((tm, a.shape[1]), lambda i: (jnp.maximum(i - nlb, 0), 0))
    sel = lambda i: jnp.where(i >= nlb, 1, 0)
    return pl.pallas_call(
        functools.partial(_out_proj_kernel, n_lat_blocks=nlb),
        grid=(M // tm,),
        in_specs=[lat(o_lat[0]), lat(o_lat[1]), lat(o_lat[2]),
                  ctx(o_ctx[0]), ctx(o_ctx[1]), ctx(o_ctx[2]),
                  pl.BlockSpec((tm, D), lambda i: (i, 0)),
                  _resident((1,) + w_out_b.shape[1:], lambda i: (layer, 0, 0)),
                  _mod_spec(layer, sel, 2)(D),
                  pl.BlockSpec((1, 1, D), lambda i: (layer, 0, 0)),
                  _mod_spec(layer, sel, 4)(D),
                  _mod_spec(layer, sel, 3)(D),
                  _resident((1, D, 2 * LANES), lambda i: (layer, 0, 0)),
                  pl.BlockSpec((1, 1, LANES), lambda i: (layer, 0, 0))],
        out_specs=[pl.BlockSpec((tm, D), lambda i: (i, 0)),
                   pl.BlockSpec((tm, D), lambda i: (i, 0)),
                   pl.BlockSpec((tm, LANES), lambda i: (i, 0))],
        out_shape=[jax.ShapeDtypeStruct((M, D), F32),
                   jax.ShapeDtypeStruct((M, D), F32),
                   jax.ShapeDtypeStruct((M, LANES), F32)],
        compiler_params=_params("parallel"),
        name="w_out",
    )(*o_lat, *o_ctx, xa, w_out_b, mods, ffn_g, mods, mods, rw, rb)


def _deinterleave_matrix():
    p = np.zeros((MXU_DIM, MXU_DIM), np.float32)
    i = np.arange(MXU_DIM // 2)
    p[2 * i, i] = 1.0
    p[2 * i + 1, MXU_DIM // 2 + i] = 1.0
    return jnp.asarray(p, BF16)


def _deint_kernel(w_ref, p_ref, o_ref):
    for c in range(w_ref.shape[-1] // MXU_DIM):
        cols = slice(c * MXU_DIM, (c + 1) * MXU_DIM)
        w = w_ref[0, 0, :, cols].astype(BF16)
        o_ref[0, :, cols] = jnp.dot(w, p_ref[...], preferred_element_type=F32).astype(BF16)


def deinterleave_gu(w_gu, layer):
    _, E, D, F2 = w_gu.shape
    tr = _pick_tile(D, 2 * ROW_BLOCK)
    return pl.pallas_call(
        _deint_kernel,
        grid=(E, D // tr),
        in_specs=[pl.BlockSpec((1, 1, tr, F2), lambda e, j: (layer, e, j, 0)),
                  pl.BlockSpec((MXU_DIM, MXU_DIM), lambda e, j: (0, 0))],
        out_specs=pl.BlockSpec((1, tr, F2), lambda e, j: (e, j, 0)),
        out_shape=jax.ShapeDtypeStruct((E, D, F2), BF16),
        compiler_params=_params("parallel", "parallel"),
        name="deint_gu",
    )(w_gu, _deinterleave_matrix())


def _moe_kernel(blk_e_ref, n_used_ref, x_ref, wgu_ref, bgu_ref, wd_ref, bd_ref, o_ref, wd_scr):
    i = pl.program_id(0)
    e = blk_e_ref[i]
    e_prev = blk_e_ref[jnp.maximum(i - 1, 0)]

    @pl.when((i == 0) | (e != e_prev))
    def _():
        wd_scr[...] = wd_ref[0, 0].astype(BF16)

    @pl.when(i < n_used_ref[0])
    def _():
        x = x_ref[...].astype(BF16)
        gu = jnp.dot(x, wgu_ref[0], preferred_element_type=F32) + bgu_ref[0]
        acts = []
        for c in range(gu.shape[1] // MXU_DIM):
            g = gu[:, c * MXU_DIM:c * MXU_DIM + LANES]
            u = gu[:, c * MXU_DIM + LANES:(c + 1) * MXU_DIM]
            glu = jnp.minimum(g, SWIGLU_LIMIT)
            lin = jnp.clip(u, -SWIGLU_LIMIT, SWIGLU_LIMIT)
            acts.append((glu * jax.nn.sigmoid(SWIGLU_ALPHA * glu) * (lin + 1.0)).astype(BF16))
        act = jnp.concatenate(acts, axis=1)
        o_ref[...] = jnp.dot(act, wd_scr[...], preferred_element_type=F32) + bd_ref[0, 0]

    @pl.when(i >= n_used_ref[0])
    def _():
        o_ref[...] = jnp.zeros_like(o_ref)


def moe_experts(xb, blk_e, n_used, wgu, bgu, w_down, b_down, layer):
    cap, D = xb.shape
    n_blocks = cap // MOE_BLOCK
    F2 = wgu.shape[-1]
    F = w_down.shape[2]
    grid_spec = pltpu.PrefetchScalarGridSpec(
        num_scalar_prefetch=2,
        grid=(n_blocks,),
        in_specs=[pl.BlockSpec((MOE_BLOCK, D), lambda i, e, n: (jnp.minimum(i, n[0] - 1), 0)),
                  pl.BlockSpec((1, D, F2), lambda i, e, n: (e[i], 0, 0)),
                  pl.BlockSpec((1, 1, F2), lambda i, e, n: (e[i], 0, 0)),
                  pl.BlockSpec((1, 1, F, D), lambda i, e, n: (layer, e[i], 0, 0)),
                  pl.BlockSpec((1, 1, 1, D), lambda i, e, n: (layer, e[i], 0, 0))],
        out_specs=pl.BlockSpec((MOE_BLOCK, D), lambda i, e, n: (i, 0)),
        scratch_shapes=[pltpu.VMEM((F, D), BF16)],
    )
    return pl.pallas_call(
        _moe_kernel,
        grid_spec=grid_spec,
        out_shape=jax.ShapeDtypeStruct((cap, D), F32),
        compiler_params=_params("arbitrary"),
        name="moe_experts",
    )(blk_e, n_used, xb, wgu, bgu, w_down, b_down)


def _combine_kernel(pc0, pc1, pc2, pc3, pn0, pn1, pn2, pn3, g_ref, x_ref, g2_ref, yb_hbm,
                    o_ref, buf, sem):
    i = pl.program_id(0)
    nb = pl.num_programs(0)
    tb = o_ref.shape[0]
    slot = i % 2

    def issue(pos_refs, s):
        for k in range(TOP_K):
            for t in range(tb):
                r = pos_refs[k][t]
                pltpu.make_async_copy(yb_hbm.at[pl.ds(r, 1), :],
                                      buf.at[s, pl.ds(k * tb + t, 1), :], sem.at[s]).start()

    @pl.when(i == 0)
    def _():
        issue((pc0, pc1, pc2, pc3), 0)

    @pl.when(i + 1 < nb)
    def _():
        issue((pn0, pn1, pn2, pn3), 1 - slot)

    pltpu.make_async_copy(yb_hbm.at[pl.ds(0, TOP_K * tb), :], buf.at[slot], sem.at[slot]).wait()
    y = jnp.zeros(o_ref.shape, F32)
    for k in range(TOP_K):
        y = y + g_ref[:, k:k + 1] * buf[slot, k * tb:(k + 1) * tb, :]
    o_ref[...] = x_ref[...] + g2_ref[0] * y


def moe_combine(yb, pos, gates, x1, mods, layer, n_lat):
    n, D = x1.shape
    tb = COMBINE_BLOCK
    nb = n // tb
    nlb = n_lat // tb
    sel = lambda i: jnp.where(i >= nlb, 1, 0)
    cur = pl.BlockSpec((tb,), lambda i: (i,), memory_space=pltpu.SMEM)
    nxt = pl.BlockSpec((tb,), lambda i: (jnp.minimum(i + 1, nb - 1),), memory_space=pltpu.SMEM)
    return pl.pallas_call(
        _combine_kernel,
        grid=(nb,),
        in_specs=[cur] * TOP_K + [nxt] * TOP_K + [
            pl.BlockSpec((tb, TOP_K), lambda i: (i, 0)),
            pl.BlockSpec((tb, D), lambda i: (i, 0)),
            _mod_spec(layer, sel, 5)(D),
            pl.BlockSpec(memory_space=pl.ANY)],
        out_specs=pl.BlockSpec((tb, D), lambda i: (i, 0)),
        out_shape=jax.ShapeDtypeStruct((n, D), F32),
        scratch_shapes=[pltpu.VMEM((2, TOP_K * tb, D), F32),
                        pltpu.SemaphoreType.DMA((2,))],
        compiler_params=_params("arbitrary"),
        name="moe_combine",
    )(pos[0], pos[1], pos[2], pos[3], pos[0], pos[1], pos[2], pos[3], gates, x1, mods, yb)


def _route_kernel(lg_ref, e_o, g_o, r_o, cnt_o, run_scr):
    i = pl.program_id(0)

    @pl.when(i == 0)
    def _():
        run_scr[...] = jnp.zeros_like(run_scr)

    tb = lg_ref.shape[0]
    lane = lax.broadcasted_iota(jnp.int32, (tb, LANES), 1)
    x = jnp.where(lane < N_EXPERTS, lg_ref[...], -jnp.inf)
    vals, hots = [], []
    for k in range(TOP_K):
        m = jnp.max(x, axis=-1, keepdims=True)
        idx = jnp.min(jnp.where(x == m, lane, LANES), axis=-1, keepdims=True)
        hot = lane == idx
        e_o[:, k:k + 1] = idx
        vals.append(m)
        hots.append(hot)
        x = jnp.where(hot, -jnp.inf, x)
    exps = [jnp.exp(v - vals[0]) for v in vals]
    denom = exps[0] + exps[1] + exps[2] + exps[3]
    for k in range(TOP_K):
        g_o[:, k:k + 1] = exps[k] / denom
    picked = (hots[0] | hots[1] | hots[2] | hots[3]).astype(BF16)
    r_idx = lax.broadcasted_iota(jnp.int32, (tb, tb), 0)
    c_idx = lax.broadcasted_iota(jnp.int32, (tb, tb), 1)
    before = (c_idx < r_idx).astype(BF16)
    seen = jnp.dot(before, picked, preferred_element_type=F32) + run_scr[...]
    for k in range(TOP_K):
        rank = jnp.sum(jnp.where(hots[k], seen, 0.0), axis=-1, keepdims=True)
        r_o[:, k:k + 1] = rank.astype(jnp.int32)
    run_scr[...] = run_scr[...] + jnp.sum(picked.astype(F32), axis=0, keepdims=True)
    cnt_o[...] = run_scr[...]


def route(logits):
    n = logits.shape[0]
    tb = ROW_BLOCK
    out = lambda: pl.BlockSpec((tb, TOP_K), lambda i: (i, 0))
    return pl.pallas_call(
        _route_kernel,
        grid=(n // tb,),
        in_specs=[pl.BlockSpec((tb, LANES), lambda i: (i, 0))],
        out_specs=[out(), out(), out(), pl.BlockSpec((1, LANES), lambda i: (0, 0))],
        out_shape=[jax.ShapeDtypeStruct((n, TOP_K), jnp.int32),
                   jax.ShapeDtypeStruct((n, TOP_K), F32),
                   jax.ShapeDtypeStruct((n, TOP_K), jnp.int32),
                   jax.ShapeDtypeStruct((1, LANES), F32)],
        scratch_shapes=[pltpu.VMEM((1, LANES), F32)],
        compiler_params=_params("arbitrary"),
        name="route",
    )(logits)


def _dispatch_kernel(pe_ref, cnt_ref, p0, p1, p2, p3, x_ref, xb_hbm, src, zeros, sem, zsem):
    i = pl.program_id(0)
    nb = pl.num_programs(0)
    tb = x_ref.shape[0]
    slot = i % 2
    pos_refs = (p0, p1, p2, p3)

    def zero_copy(row0):
        return pltpu.make_async_copy(zeros, xb_hbm.at[pl.ds(row0, MOE_BLOCK), :], zsem)

    @pl.when(i == 0)
    def _():
        zeros[...] = jnp.zeros_like(zeros)
        for e in range(N_EXPERTS):
            @pl.when(cnt_ref[e] > 0)
            def _(e=e):
                zero_copy(pl.multiple_of(pe_ref[e] - MOE_BLOCK, MOE_BLOCK)).start()
        n_used = pe_ref[N_EXPERTS - 1] // MOE_BLOCK
        n_blocks = xb_hbm.shape[0] // MOE_BLOCK

        def fill(b, carry):
            zero_copy(pl.multiple_of(b * MOE_BLOCK, MOE_BLOCK)).start()
            return carry

        def drain(b, carry):
            zero_copy(0).wait()
            return carry

        lax.fori_loop(n_used, n_blocks, fill, 0)
        lax.fori_loop(n_used, n_blocks, drain, 0)
        for e in range(N_EXPERTS):
            @pl.when(cnt_ref[e] > 0)
            def _():
                zero_copy(0).wait()

    def wait_slot(s):
        for _ in range(TOP_K):
            pltpu.make_async_copy(src.at[s], xb_hbm.at[pl.ds(0, tb), :], sem.at[s]).wait()

    @pl.when(i >= 2)
    def _():
        wait_slot(slot)

    src[slot] = x_ref[...]
    for k in range(TOP_K):
        for t in range(tb):
            r = pos_refs[k][t]
            pltpu.make_async_copy(src.at[slot, pl.ds(t, 1), :], xb_hbm.at[pl.ds(r, 1), :],
                                  sem.at[slot]).start()

    @pl.when(i == nb - 1)
    def _():
        wait_slot(slot)

    @pl.when((i == nb - 1) & (i >= 1))
    def _():
        wait_slot(1 - slot)


def moe_dispatch(h2, pos, pad_end, counts, cap):
    n, D = h2.shape
    tb = COMBINE_BLOCK
    grid_spec = pltpu.PrefetchScalarGridSpec(
        num_scalar_prefetch=2,
        grid=(n // tb,),
        in_specs=[pl.BlockSpec((tb,), lambda i, pe, c: (i,), memory_space=pltpu.SMEM)] * TOP_K + [
            pl.BlockSpec((tb, D), lambda i, pe, c: (i, 0))],
        out_specs=pl.BlockSpec(memory_space=pl.ANY),
        scratch_shapes=[pltpu.VMEM((2, tb, D), F32),
                        pltpu.VMEM((MOE_BLOCK, D), F32),
                        pltpu.SemaphoreType.DMA((2,)),
                        pltpu.SemaphoreType.DMA(())],
    )
    return pl.pallas_call(
        _dispatch_kernel,
        grid_spec=grid_spec,
        out_shape=jax.ShapeDtypeStruct((cap, D), F32),
        compiler_params=_params("arbitrary"),
        name="moe_dispatch",
    )(pad_end, counts, pos[0], pos[1], pos[2], pos[3], h2)


def moe_ffn(x1, h2, logits, wgu, bgu, w_down, b_down, mods, layer, n_lat):
    n, D = h2.shape
    top_e, gates, rank, cnt = route(logits)
    n_assign = n * TOP_K
    counts = cnt[0, :N_EXPERTS].astype(jnp.int32)
    padded = (counts + MOE_BLOCK - 1) // MOE_BLOCK * MOE_BLOCK
    pad_end = jnp.cumsum(padded)
    pad_start = pad_end - padded
    hot = top_e[:, :, None] == jnp.arange(N_EXPERTS, dtype=jnp.int32)[None, None, :]
    pos = (rank + jnp.sum(jnp.where(hot, pad_start[None, None, :], 0), axis=-1)).T
    n_blocks = -(-n_assign // MOE_BLOCK) + N_EXPERTS
    cap = n_blocks * MOE_BLOCK
    blk_row0 = jnp.arange(n_blocks, dtype=jnp.int32) * MOE_BLOCK
    blk_e = jnp.minimum(jnp.sum((pad_end[None, :] <= blk_row0[:, None]).astype(jnp.int32), axis=1),
                        N_EXPERTS - 1)
    n_used = pad_end[-1:] // MOE_BLOCK
    xb = moe_dispatch(h2, pos, pad_end, counts, cap)
    yb = moe_experts(xb, blk_e, n_used, wgu, bgu, w_down, b_down, layer)
    return moe_combine(yb, pos, gates, x1, mods, layer, n_lat)


def _rope_tables(n_lat, n_ctx):
    quarter = ROT_DIM // 4
    inv = 1.0 / (ROPE_THETA ** (jnp.arange(quarter, dtype=F32) / quarter))
    t = jnp.arange(n_lat)
    row = (t // GRID_W).astype(F32)
    col = (t % GRID_W).astype(F32)
    ang = jnp.concatenate([row[:, None] * inv, col[:, None] * inv], axis=-1)
    ang = jnp.concatenate([ang, jnp.zeros((n_ctx, ROT_DIM // 2), F32)], axis=0)
    cos, sin = jnp.cos(ang), jnp.sin(ang)
    reps = LANES // ROT_DIM
    cos_t = jnp.tile(jnp.concatenate([cos, cos], axis=-1), (1, reps))
    sin_t = jnp.tile(jnp.concatenate([-sin, sin], axis=-1), (1, reps))
    return cos_t, sin_t


def _pad_w_in(w):
    L, D, _ = w.shape
    a = MLA_Q_RANK + MLA_KV_RANK + MLA_ROPE
    return jnp.concatenate([w[..., :a], jnp.zeros((L, D, _P_DQ - a), w.dtype), w[..., a:]], axis=-1)


def _pad_w_uq(w):
    L, r, _ = w.shape
    w4 = w.reshape(L, r, MLA_HEADS, MLA_NOPE + MLA_ROPE)
    w4 = jnp.concatenate(
        [w4, jnp.zeros((L, r, MLA_HEADS, MLA_HEAD_PAD - MLA_NOPE - MLA_ROPE), w.dtype)], axis=-1)
    return w4.reshape(L, r, MLA_HEADS * MLA_HEAD_PAD)


def _row3(a, width=None, tile=1):
    a = jnp.tile(a.astype(F32), (1, tile))
    if width is not None and width > a.shape[1]:
        a = jnp.concatenate([a, jnp.zeros((a.shape[0], width - a.shape[1]), F32)], axis=1)
    return a[:, None, :]


def kernel(x, c, ctx, c_ctx, w_ada, b_ada, attn_norm, ffn_norm, w_in, mla_qa_norm, w_uq,
           mla_kva_norm, w_ukv, mla_q_gain, mla_knope_gain, mla_kpe_gain, diff_q_gain, diff_k_gain,
           diff_lambda, diff_subln, na_q_gain, na_k_gain, na_rpb, w_out, router_w, router_b,
           w_gu, b_gu, w_down, b_down):
    depth = w_in.shape[0]
    B, N, D = x.shape
    assert B == 1
    C = ctx.shape[1]
    T = N + C
    assert N % ROW_BLOCK == 0 and C % ROW_BLOCK == 0
    rows = N // GRID_W
    cos_t, sin_t = _rope_tables(N, C)
    silu = jnp.stack([jax.nn.silu(c[0]), jax.nn.silu(c_ctx)])
    silu_pad = jnp.concatenate([silu, jnp.zeros((14, D), F32)], axis=0).astype(BF16)
    n_groups = w_gu.shape[-1] // MXU_DIM

    mods = ada_all(silu_pad, w_ada)[:, :2] + b_ada[:, None, :]
    mods = mods.reshape(depth * 2 * N_MOD, 1, D)

    w_in_p = _pad_w_in(w_in).astype(BF16)
    w_uq_p = _pad_w_uq(w_uq).astype(BF16)
    w_ukv_b = w_ukv.astype(BF16)
    w_out_b = w_out.astype(BF16)
    gains = (_row3(mla_qa_norm), _row3(mla_kva_norm), _row3(mla_q_gain, MLA_HEAD_PAD),
             _row3(mla_knope_gain), _row3(mla_kpe_gain, LANES),
             _row3(diff_q_gain, tile=2), _row3(diff_k_gain, tile=2),
             _row3(na_q_gain), _row3(na_k_gain))
    attn_g = _row3(attn_norm)
    ffn_g = _row3(ffn_norm)
    subln = _row3(diff_subln)
    rw = jnp.concatenate([router_w.astype(F32),
                          jnp.zeros((depth, D, LANES - N_EXPERTS), F32)], axis=-1)
    rw_hi = rw.astype(BF16)
    rw = jnp.concatenate([rw_hi, (rw - rw_hi.astype(F32)).astype(BF16)], axis=-1)
    rb = _row3(router_b, LANES)
    bgu_all = b_gu.astype(F32).reshape(depth, N_EXPERTS, n_groups, LANES, 2)
    bgu_all = bgu_all.transpose(0, 1, 2, 4, 3).reshape(depth, N_EXPERTS, 1, n_groups * MXU_DIM)
    bd_all = b_down.astype(F32)[:, :, None, :]
    na_bias = _na_bias_tables(na_rpb, rows)

    xa = jnp.concatenate([x[0], ctx[0]], axis=0)
    for l in range(depth):
        with_ctx_out = l < depth - 1
        lam_init = 0.8 - 0.6 * math.exp(-0.3 * l)
        q_a, k_a, v_a, q_b, k_b, v_b, q_c, k_c, v_c = in_proj(
            xa, attn_g, mods, w_in_p, cos_t, sin_t, gains, w_uq_p, w_ukv_b, l, N)

        lf = diff_lambda[l].astype(F32)
        lam = jnp.exp(jnp.sum(lf[0] * lf[1])) - jnp.exp(jnp.sum(lf[2] * lf[3])) + lam_init
        diff_args = (jnp.full((1, LANES), lam, F32), subln, l, 1.0 - lam_init)

        o_a = flash(q_a, k_a, v_a, heads=MLA_HEADS, dk=MLA_HEAD_PAD,
                    n_q=N, q_start=0, n_k=T, k_start=0, tq_target=512, tk_target=1280, name="mla")
        o_b = flash(q_b, k_b, v_b, heads=DIFF_HEADS, dk=2 * DIFF_QK,
                    n_q=N, q_start=0, n_k=T, k_start=0, tq_target=256, tk_target=1280,
                    diff_args=diff_args, name="diff")
        o_c = neighborhood(q_c, k_c, v_c, na_bias, N, l)
        o_ctx = None
        wgu = deinterleave_gu(w_gu, l)
        if with_ctx_out:
            v_cc = _with_ones_column(v_c[N:].reshape(C, NA_HEADS, NA_DIM))
            oc_a = flash(q_a, k_a, v_a, heads=MLA_HEADS, dk=MLA_HEAD_PAD,
                         n_q=C, q_start=N, n_k=C, k_start=N, tq_target=256, tk_target=256, name="mla_ctx")
            oc_b = flash(q_b, k_b, v_b, heads=DIFF_HEADS, dk=2 * DIFF_QK,
                         n_q=C, q_start=N, n_k=C, k_start=N, tq_target=256, tk_target=256,
                         diff_args=diff_args, name="diff_ctx")
            oc_c = flash(q_c, k_c[N:], v_cc, heads=NA_HEADS, dk=NA_DIM,
                         n_q=C, q_start=N, n_k=C, k_start=0, tq_target=256, tk_target=256, name="na_ctx")
            o_ctx = (oc_a, oc_b, oc_c)
        x1, h2, logits = out_proj((o_a, o_b, o_c), o_ctx, xa, w_out_b, mods, ffn_g, rw, rb, l)
        xa = moe_ffn(x1, h2, logits, wgu, bgu_all[l], w_down, bd_all, mods, l, N)
    return xa[:N][None]
```

```python
import functools
import math

import numpy as np
import jax
import jax.numpy as jnp
from jax import lax
from jax.experimental import pallas as pl
from jax.experimental.pallas import tpu as pltpu

F32 = jnp.float32
BF16 = jnp.bfloat16

GRID_W = 64
EPS = 1e-6
ROPE_THETA = 10000.0
ROT_DIM = 64
MLA_HEADS = 8
MLA_NOPE = 128
MLA_ROPE = 64
MLA_V = 128
MLA_Q_RANK = 512
MLA_KV_RANK = 256
MLA_HEAD_PAD = 256
DIFF_HEADS = 4
DIFF_QK = 64
DIFF_V = 128
NA_HEADS = 4
NA_DIM = 128
NA_KH = 8
NA_KW = 16
NA_ROWS_PER_GROUP = 4
N_EXPERTS = 32
TOP_K = 4
D_EXPERT = 768
SWIGLU_ALPHA = 1.702
SWIGLU_LIMIT = 7.0
MOE_BLOCK = 256
COMBINE_BLOCK = 128
ROW_BLOCK = 256
N_MOD = 6
NEG_BIG = -1e30
LOG2E = math.log2(math.e)

LANES = 128
MXU_DIM = 256
VMEM_LIMIT = 56 * 1024 * 1024

NT_DIMS = (((1,), (1,)), ((), ()))

_P_QA, _P_KVA, _P_KPE, _P_DQ, _P_DK, _P_DV, _P_NQ, _P_NK, _P_NV, _P_END = (
    0, 512, 768, 1024, 1536, 2048, 2560, 3072, 3584, 4096)


def _pick_tile(m, target, mult=8):
    best = None
    for t in range(mult, min(m, target) + 1, mult):
        if m % t == 0:
            best = t
    assert best is not None, (m, target)
    return best


def _params(*sem):
    return pltpu.CompilerParams(dimension_semantics=sem, vmem_limit_bytes=VMEM_LIMIT)


def _resident(block_shape, index_map):
    return pl.BlockSpec(block_shape, index_map, pipeline_mode=pl.Buffered(1))


def _mod_spec(layer, sel_fn, k):
    return lambda D: pl.BlockSpec((1, 1, D), lambda i: ((layer * 2 + sel_fn(i)) * N_MOD + k, 0, 0))


def _ada_kernel(a_ref, w_ref, o_ref):
    o_ref[0] = jnp.dot(a_ref[...], w_ref[0].astype(BF16), preferred_element_type=F32)


def ada_all(silu_pad, w_ada):
    L, D, N6 = w_ada.shape
    tn = _pick_tile(N6, 512, LANES)
    return pl.pallas_call(
        _ada_kernel,
        grid=(L, N6 // tn),
        in_specs=[pl.BlockSpec((16, D), lambda l, j: (0, 0)),
                  pl.BlockSpec((1, D, tn), lambda l, j: (l, 0, j))],
        out_specs=pl.BlockSpec((1, 16, tn), lambda l, j: (l, 0, j)),
        out_shape=jax.ShapeDtypeStruct((L, 16, N6), F32),
        compiler_params=_params("parallel", "parallel"),
        name="ada",
    )(silu_pad, w_ada)


def _norm_mm_kernel(x_ref, g_ref, sc_ref, sh_ref, w_ref, o_ref):
    x = x_ref[...]
    ms = jnp.mean(x * x, axis=-1, keepdims=True)
    h = x * lax.rsqrt(ms + EPS) * g_ref[0] * (1.0 + sc_ref[0]) + sh_ref[0]
    o_ref[...] = jnp.dot(h.astype(BF16), w_ref[0], preferred_element_type=F32)


def _post_proj_kernel(p_ref, cos_ref, sin_ref, gqa_ref, gkva_ref, gq_ref, gkn_ref, gkpe_ref,
                      gdq_ref, gdk_ref, gnq_ref, gnk_ref, wuq_ref, wukv_ref,
                      qa_o, ka_o, va_o, qb_o, kb_o, vb_o, qc_o, kc_o, vc_o,
                      *, mla_scale, diff_scale, na_scale):
    tm = p_ref.shape[0]
    cos_t = cos_ref[...]
    sin_t = sin_ref[...]
    lane = lax.broadcasted_iota(jnp.int32, (tm, LANES), 1)
    lo = lane < ROT_DIM
    first_half = (lane & (ROT_DIM - 1)) < ROT_DIM // 2
    ones_col = jnp.where(lane == 0, 1.0, 0.0).astype(BF16)

    def rope(x):
        sw = jnp.where(first_half, pltpu.roll(x, LANES - ROT_DIM // 2, 1),
                       pltpu.roll(x, ROT_DIM // 2, 1))
        return x * cos_t + sw * sin_t

    def rinv(x, n):
        return lax.rsqrt(jnp.sum(x * x, axis=-1, keepdims=True) * (1.0 / n) + EPS)

    qa = p_ref[:, _P_QA:_P_KVA]
    qa_n = (qa * rinv(qa, MLA_Q_RANK) * gqa_ref[0]).astype(BF16)
    qf = jnp.dot(qa_n, wuq_ref[0], preferred_element_type=F32)
    g_q = gq_ref[0]
    for h in range(MLA_HEADS):
        c0 = h * MLA_HEAD_PAD
        blk = qf[:, c0:c0 + MLA_HEAD_PAD]
        r = rinv(blk, MLA_NOPE + MLA_ROPE) * mla_scale
        qa_o[:, c0:c0 + LANES] = (blk[:, :LANES] * r * g_q[:, :LANES]).astype(BF16)
        qa_o[:, c0 + LANES:c0 + 2 * LANES] = rope(blk[:, LANES:] * r * g_q[:, LANES:]).astype(BF16)

    kva = p_ref[:, _P_KVA:_P_KPE]
    kva_n = (kva * rinv(kva, MLA_KV_RANK) * gkva_ref[0]).astype(BF16)
    kv = jnp.dot(kva_n, wukv_ref[0], preferred_element_type=F32)
    kpe = p_ref[:, _P_KPE:_P_KPE + LANES]
    kpe_r = rope(kpe * rinv(kpe, MLA_ROPE) * gkpe_ref[0]).astype(BF16)
    for h in range(MLA_HEADS):
        c0 = h * MLA_HEAD_PAD
        kn = kv[:, c0:c0 + LANES]
        ka_o[:, c0:c0 + LANES] = (kn * rinv(kn, MLA_NOPE) * gkn_ref[0]).astype(BF16)
        ka_o[:, c0 + LANES:c0 + 2 * LANES] = kpe_r
        va_o[:, c0:c0 + LANES] = kv[:, c0 + LANES:c0 + 2 * LANES].astype(BF16)
        va_o[:, c0 + LANES:c0 + 2 * LANES] = ones_col

    def subhead_norm_rope(x, g_ref, scale):
        sq = x * x
        s_lo = jnp.sum(jnp.where(lo, sq, 0.0), axis=-1, keepdims=True)
        s_hi = jnp.sum(jnp.where(lo, 0.0, sq), axis=-1, keepdims=True)
        r = jnp.where(lo, lax.rsqrt(s_lo * (1.0 / DIFF_QK) + EPS),
                      lax.rsqrt(s_hi * (1.0 / DIFF_QK) + EPS))
        return (rope(x * r * g_ref[0]) * scale).astype(BF16)

    for h in range(DIFF_HEADS):
        c0 = h * LANES
        qb_o[:, c0:c0 + LANES] = subhead_norm_rope(p_ref[:, _P_DQ + c0:_P_DQ + c0 + LANES], gdq_ref, diff_scale)
        kb_o[:, c0:c0 + LANES] = subhead_norm_rope(p_ref[:, _P_DK + c0:_P_DK + c0 + LANES], gdk_ref, 1.0)
        vb_o[:, 2 * c0:2 * c0 + LANES] = p_ref[:, _P_DV + c0:_P_DV + c0 + LANES].astype(BF16)
        vb_o[:, 2 * c0 + LANES:2 * c0 + 2 * LANES] = ones_col

    for h in range(NA_HEADS):
        c0 = h * NA_DIM
        x = p_ref[:, _P_NQ + c0:_P_NQ + c0 + NA_DIM]
        qc_o[:, c0:c0 + NA_DIM] = (x * rinv(x, NA_DIM) * gnq_ref[0] * na_scale).astype(BF16)
        x = p_ref[:, _P_NK + c0:_P_NK + c0 + NA_DIM]
        kc_o[:, c0:c0 + NA_DIM] = (x * rinv(x, NA_DIM) * gnk_ref[0]).astype(BF16)
    vc_o[...] = p_ref[:, _P_NV:_P_END].astype(BF16)


def _in_proj_kernel(x_ref, g_ref, sc_ref, sh_ref, w_ref, *rest, **scales):
    p_scr = rest[-1]
    _norm_mm_kernel(x_ref, g_ref, sc_ref, sh_ref, w_ref, p_scr)
    _post_proj_kernel(p_scr, *rest[:-1], **scales)


def in_proj(xa, norm_g, mods, w_in_p, cos_t, sin_t, gains, w_uq_p, w_ukv_b, layer, n_lat):
    T, D = xa.shape
    P = w_in_p.shape[-1]
    tm = ROW_BLOCK
    nlb = n_lat // tm
    sel = lambda i: jnp.where(i >= nlb, 1, 0)
    row = lambda w: pl.BlockSpec((tm, w), lambda i: (i, 0))
    gain_specs = [pl.BlockSpec((1, 1, g.shape[-1]), lambda i: (layer, 0, 0)) for g in gains]
    widths = (MLA_HEADS * MLA_HEAD_PAD, MLA_HEADS * MLA_HEAD_PAD, MLA_HEADS * 2 * LANES,
              DIFF_HEADS * LANES, DIFF_HEADS * LANES, DIFF_HEADS * 2 * LANES,
              NA_HEADS * NA_DIM, NA_HEADS * NA_DIM, NA_HEADS * NA_DIM)
    kern = functools.partial(
        _in_proj_kernel,
        mla_scale=(MLA_NOPE + MLA_ROPE) ** -0.5 * LOG2E,
        diff_scale=DIFF_QK ** -0.5 * LOG2E,
        na_scale=NA_DIM ** -0.5 * LOG2E)
    return pl.pallas_call(
        kern,
        grid=(T // tm,),
        in_specs=[row(D),
                  pl.BlockSpec((1, 1, D), lambda i: (layer, 0, 0)),
                  _mod_spec(layer, sel, 1)(D),
                  _mod_spec(layer, sel, 0)(D),
                  _resident((1, D, P), lambda i: (layer, 0, 0)),
                  row(LANES), row(LANES)] + gain_specs + [
            _resident((1,) + w_uq_p.shape[1:], lambda i: (layer, 0, 0)),
            _resident((1,) + w_ukv_b.shape[1:], lambda i: (layer, 0, 0))],
        out_specs=[row(w) for w in widths],
        out_shape=[jax.ShapeDtypeStruct((T, w), BF16) for w in widths],
        scratch_shapes=[pltpu.VMEM((tm, P), F32)],
        compiler_params=_params("parallel"),
        name="w_in",
    )(xa, norm_g, mods, mods, w_in_p, cos_t, sin_t, *gains, w_uq_p, w_ukv_b)


def _flash_kernel(*refs, tk, n_chunks, diff, lam_scale):
    if diff:
        q_ref, k_ref, v_ref, lam_ref, g_ref, o_ref, q_scr, s_scr, acc_scr = refs
    else:
        q_ref, k_ref, v_ref, o_ref, q_scr, s_scr, acc_scr = refs
    n_sub, rows, _ = q_scr.shape
    tq = q_ref.shape[0] // n_sub
    for u in range(n_sub):
        q = q_ref[u * tq:(u + 1) * tq, :]
        if diff:
            lane = lax.broadcasted_iota(jnp.int32, q.shape, 1)
            zero = jnp.zeros_like(q)
            q_scr[u, :tq, :] = jnp.where(lane < DIFF_QK, q, zero)
            q_scr[u, tq:, :] = jnp.where(lane >= DIFF_QK, q, zero)
        else:
            q_scr[u] = q
    acc_scr[...] = jnp.zeros_like(acc_scr)

    def qk(u, c, slot):
        s_scr[u, slot] = lax.dot_general(q_scr[u], k_ref[c * tk:(c + 1) * tk, :], NT_DIMS,
                                         preferred_element_type=F32)

    def consume(u, c, slot, m):
        s = s_scr[u, slot]
        m_new = jnp.maximum(m, jnp.max(s, axis=-1, keepdims=True))
        alpha = jnp.exp2(m - m_new)
        p = jnp.exp2(s - m_new).astype(BF16)
        acc_scr[u] = alpha * acc_scr[u] + jnp.dot(
            p, v_ref[c * tk:(c + 1) * tk, :], preferred_element_type=F32)
        return m_new

    ms = [jnp.full((rows, 1), -jnp.inf, F32) for _ in range(n_sub)]
    for u in range(n_sub):
        qk(u, 0, 0)
    for c in range(n_chunks):
        for u in range(n_sub):
            if c + 1 < n_chunks:
                qk(u, c + 1, (c + 1) % 2)
            ms[u] = consume(u, c, c % 2, ms[u])
    for u in range(n_sub):
        acc = acc_scr[u]
        o = acc[:, :LANES] / acc[:, LANES:LANES + 1]
        if diff:
            o = o[:tq] - lam_ref[...] * o[tq:]
            msq = jnp.mean(o * o, axis=-1, keepdims=True)
            o = o * lax.rsqrt(msq + EPS) * g_ref[0] * lam_scale
        o_ref[u * tq:(u + 1) * tq, :] = o.astype(o_ref.dtype)


def flash(q, k, v, *, heads, dk, n_q, q_start, n_k, k_start, tq_target, tk_target,
          diff_args=None, n_sub=1, name="flash"):
    tq = _pick_tile(math.gcd(n_q, q_start) if q_start else n_q, tq_target)
    tk = _pick_tile(n_k, tk_target, LANES)
    assert k_start % n_k == 0
    q_blk0 = q_start // tq
    k_blk0 = k_start // n_k
    vw = 2 * LANES
    in_specs = [pl.BlockSpec((tq, dk), lambda h, i: (q_blk0 + i, h)),
                pl.BlockSpec((n_k, dk), lambda h, i: (k_blk0, h)),
                pl.BlockSpec((n_k, vw), lambda h, i: (k_blk0, h))]
    args = [q, k, v]
    diff = diff_args is not None
    lam_scale = 1.0
    if diff:
        lam_row, subln, layer, lam_scale = diff_args
        in_specs += [pl.BlockSpec((1, LANES), lambda h, i: (0, 0)),
                     pl.BlockSpec((1, 1, LANES), lambda h, i: (layer, 0, 0))]
        args += [lam_row, subln]
    rows = (2 if diff else 1) * tq // n_sub
    kern = functools.partial(_flash_kernel, tk=tk, n_chunks=n_k // tk, diff=diff,
                             lam_scale=lam_scale)
    return pl.pallas_call(
        kern,
        grid=(heads, n_q // tq),
        in_specs=in_specs,
        out_specs=pl.BlockSpec((tq, LANES), lambda h, i: (i, h)),
        out_shape=jax.ShapeDtypeStruct((n_q, heads * LANES), BF16),
        scratch_shapes=[pltpu.VMEM((n_sub, rows, dk), BF16),
                        pltpu.VMEM((n_sub, 2, rows, tk), F32),
                        pltpu.VMEM((n_sub, rows, vw), F32)],
        compiler_params=_params("parallel", "arbitrary"),
        name=name,
    )(*args)


def _with_ones_column(v3):
    T, H, d = v3.shape
    pad = jnp.zeros((T, H, 2 * LANES - d), BF16).at[:, :, 0].set(1.0)
    return jnp.concatenate([v3.astype(BF16), pad], axis=-1).reshape(T, H * 2 * LANES)


def _na_kernel(q_ref, k0_ref, k1_ref, k2_ref, v0_ref, v1_ref, v2_ref, kc_ref, vc_ref, bias_ref,
               o_ref):
    k_refs = (k0_ref, k1_ref, k2_ref, kc_ref)
    v_refs = (v0_ref, v1_ref, v2_ref, vc_ref)
    blk = k0_ref.shape[0]
    for h in range(NA_HEADS):
        cols = slice(h * NA_DIM, (h + 1) * NA_DIM)
        q = q_ref[:, cols]
        s_parts = []
        for b in range(4):
            s = lax.dot_general(q, k_refs[b][:, cols], NT_DIMS, preferred_element_type=F32)
            if b < 3:
                s = s + bias_ref[0, 0, h, :, b * blk:(b + 1) * blk]
            s_parts.append(s)
        m = s_parts[3].max(axis=-1, keepdims=True)
        for b in range(3):
            m = jnp.maximum(m, s_parts[b].max(axis=-1, keepdims=True))
        l = jnp.zeros_like(m)
        acc = jnp.zeros((q.shape[0], NA_DIM), F32)
        for b in range(4):
            p = jnp.exp2(s_parts[b] - m)
            l = l + p.sum(axis=-1, keepdims=True)
            acc = acc + jnp.dot(p.astype(BF16), v_refs[b][:, cols], preferred_element_type=F32)
        o_ref[:, cols] = (acc / l).astype(o_ref.dtype)


def _na_bias_tables(rpb, rows):
    R = NA_ROWS_PER_GROUP
    G = rows // R
    assert rows % R == 0 and G >= 3 and rows >= NA_KH and NA_KH == 2 * R
    n_rr, n_cr = 2 * NA_KH - 1, 2 * NA_KW - 1
    qc = np.arange(GRID_W)[:, None]
    kc = np.arange(GRID_W)[None, :]
    wc = np.clip(qc - NA_KW // 2, 0, GRID_W - NA_KW)
    valid_c = (kc >= wc) & (kc < wc + NA_KW)
    col_rel = np.clip(kc - qc + NA_KW - 1, 0, n_cr - 1).reshape(-1)
    c_sel = (np.arange(n_cr)[:, None] == col_rel[None, :]).astype(np.float32)
    tabs = []
    for g in (0, 1, G - 1):
        j = np.arange(R)[:, None, None]
        b = np.arange(3)[None, :, None]
        kr = np.arange(R)[None, None, :]
        r = R * g + j
        rs = np.clip(r - NA_KH // 2, 0, rows - NA_KH)
        keyrow = R * (g - 1 + b) + kr
        valid_r = np.broadcast_to((keyrow >= rs) & (keyrow < rs + NA_KH), (R, 3, R))
        row_rel = np.broadcast_to(np.clip(keyrow - r + NA_KH - 1, 0, n_rr - 1), (R, 3, R)).reshape(-1)
        r_sel = (row_rel[:, None] == np.arange(n_rr)[None, :]).astype(np.float32)
        t = jnp.einsum('ar,lhrc,cq->lhaq', r_sel, rpb.astype(F32), c_sel,
                       precision=lax.Precision.HIGHEST)
        L, H = t.shape[:2]
        t = t.reshape(L, H, R, 3, R, GRID_W, GRID_W).transpose(0, 1, 2, 5, 3, 4, 6)
        valid = valid_r[:, None, :, :, None] & valid_c[None, :, None, None, :]
        t = jnp.where(valid[None, None], t * LOG2E, NEG_BIG)
        tabs.append(t.reshape(L, H, R * GRID_W, 3 * R * GRID_W))
    return jnp.stack(tabs, axis=1)


def neighborhood(q, k, v, bias_tabs, n_lat, layer):
    blk = NA_ROWS_PER_GROUP * GRID_W
    G = n_lat // blk
    n_ctx = q.shape[0] - n_lat
    assert n_ctx == blk, "context length must equal one key block"
    W = NA_HEADS * NA_DIM

    def kv_spec(off):
        return pl.BlockSpec((blk, W), lambda g: (jnp.clip(g + off, 0, G - 1), 0))

    ctx_spec = pl.BlockSpec((blk, W), lambda g: (G, 0))
    bias_spec = pl.BlockSpec(
        (1, 1, NA_HEADS, blk, 3 * blk),
        lambda g: (layer, jnp.where(g == 0, 0, jnp.where(g == G - 1, 2, 1)), 0, 0, 0))
    return pl.pallas_call(
        _na_kernel,
        grid=(G,),
        in_specs=[pl.BlockSpec((blk, W), lambda g: (g, 0)),
                  kv_spec(-1), kv_spec(0), kv_spec(1),
                  kv_spec(-1), kv_spec(0), kv_spec(1),
                  ctx_spec, ctx_spec, bias_spec],
        out_specs=pl.BlockSpec((blk, W), lambda g: (g, 0)),
        out_shape=jax.ShapeDtypeStruct((n_lat, W), BF16),
        compiler_params=_params("parallel"),
        name="na",
    )(q, k, k, k, v, v, v, k, v, bias_tabs)


def _out_proj_kernel(oa_ref, ob_ref, oc_ref, oa2_ref, ob2_ref, oc2_ref, x_ref, w_ref, g1_ref,
                     gn_ref, sc_ref, sh_ref, rw_ref, rb_ref, x1_o, h2_o, lg_o, *, n_lat_blocks):
    is_ctx = pl.program_id(0) >= n_lat_blocks
    wa = oa_ref.shape[1]
    wb = ob_ref.shape[1]
    oa = jnp.where(is_ctx, oa2_ref[...], oa_ref[...])
    ob = jnp.where(is_ctx, ob2_ref[...], ob_ref[...])
    oc = jnp.where(is_ctx, oc2_ref[...], oc_ref[...])
    o = jnp.dot(oa, w_ref[0, 0:wa, :], preferred_element_type=F32)
    o = o + jnp.dot(ob, w_ref[0, wa:wa + wb, :], preferred_element_type=F32)
    o = o + jnp.dot(oc, w_ref[0, wa + wb:, :], preferred_element_type=F32)
    x1 = x_ref[...] + g1_ref[0] * o
    x1_o[...] = x1
    ms = jnp.mean(x1 * x1, axis=-1, keepdims=True)
    h2 = x1 * lax.rsqrt(ms + EPS) * gn_ref[0] * (1.0 + sc_ref[0]) + sh_ref[0]
    h2_o[...] = h2
    h2_hi = h2.astype(BF16)
    h2_lo = (h2 - h2_hi.astype(F32)).astype(BF16)
    a = (jnp.dot(h2_hi, rw_ref[0], preferred_element_type=F32)
         + jnp.dot(h2_lo, rw_ref[0], preferred_element_type=F32))
    lg_o[...] = a[:, :LANES] + a[:, LANES:] + rb_ref[0]


def out_proj(o_lat, o_ctx, xa, w_out_b, mods, ffn_g, rw, rb, layer):
    n_lat = o_lat[0].shape[0]
    M = n_lat + (o_ctx[0].shape[0] if o_ctx is not None else 0)
    if o_ctx is None:
        o_ctx = o_lat
    D = xa.shape[1]
    tm = ROW_BLOCK
    nlb = n_lat // tm
    lat = lambda a: pl.BlockSpec((tm, a.shape[1]), lambda i: (jnp.minimum(i, nlb - 1), 0))
    ctx = lambda a: pl.BlockSpec((tm, a.shape[1]), lambda i: (jnp.maximum(i - nlb, 0), 0))
    sel = lambda i: jnp.where(i >= nlb, 1, 0)
    return pl.pallas_call(
        functools.partial(_out_proj_kernel, n_lat_blocks=nlb),
        grid=(M // tm,),
        in_specs=[lat(o_lat[0]), lat(o_lat[1]), lat(o_lat[2]),
                  ctx(o_ctx[0]), ctx(o_ctx[1]), ctx(o_ctx[2]),
                  pl.BlockSpec((tm, D), lambda i: (i, 0)),
                  _resident((1,) + w_out_b.shape[1:], lambda i: (layer, 0, 0)),
                  _mod_spec(layer, sel, 2)(D),
                  pl.BlockSpec((1, 1, D), lambda i: (layer, 0, 0)),
                  _mod_spec(layer, sel, 4)(D),
                  _mod_spec(layer, sel, 3)(D),
                  _resident((1, D, 2 * LANES), lambda i: (layer, 0, 0)),
                  pl.BlockSpec((1, 1, LANES), lambda i: (layer, 0, 0))],
        out_specs=[pl.BlockSpec((tm, D), lambda i: (i, 0)),
                   pl.BlockSpec((tm, D), lambda i: (i, 0)),
                   pl.BlockSpec((tm, LANES), lambda i: (i, 0))],
        out_shape=[jax.ShapeDtypeStruct((M, D), F32),
                   jax.ShapeDtypeStruct((M, D), F32),
                   jax.ShapeDtypeStruct((M, LANES), F32)],
        compiler_params=_params("parallel"),
        name="w_out",
    )(*o_lat, *o_ctx, xa, w_out_b, mods, ffn_g, mods, mods, rw, rb)


def _deinterleave_matrix():
    p = np.zeros((MXU_DIM, MXU_DIM), np.float32)
    i = np.arange(MXU_DIM // 2)
    p[2 * i, i] = 1.0
    p[2 * i + 1, MXU_DIM // 2 + i] = 1.0
    return jnp.asarray(p, BF16)


def _deint_kernel(w_ref, p_ref, o_ref):
    for c in range(w_ref.shape[-1] // MXU_DIM):
        cols = slice(c * MXU_DIM, (c + 1) * MXU_DIM)
        w = w_ref[0, 0, :, cols].astype(BF16)
        o_ref[0, :, cols] = jnp.dot(w, p_ref[...], preferred_element_type=F32).astype(BF16)


def deinterleave_gu(w_gu, layer):
    _, E, D, F2 = w_gu.shape
    tr = _pick_tile(D, 2 * ROW_BLOCK)
    return pl.pallas_call(
        _deint_kernel,
        grid=(E, D // tr),
        in_specs=[pl.BlockSpec((1, 1, tr, F2), lambda e, j: (layer, e, j, 0)),
                  pl.BlockSpec((MXU_DIM, MXU_DIM), lambda e, j: (0, 0))],
        out_specs=pl.BlockSpec((1, tr, F2), lambda e, j: (e, j, 0)),
        out_shape=jax.ShapeDtypeStruct((E, D, F2), BF16),
        compiler_params=_params("parallel", "parallel"),
        name="deint_gu",
    )(w_gu, _deinterleave_matrix())


def _moe_kernel(blk_e_ref, n_used_ref, x_ref, wgu_ref, bgu_ref, wd_ref, bd_ref, o_ref, wd_scr):
    i = pl.program_id(0)
    e = blk_e_ref[i]
    e_prev = blk_e_ref[jnp.maximum(i - 1, 0)]

    @pl.when((i == 0) | (e != e_prev))
    def _():
        wd_scr[...] = wd_ref[0, 0].astype(BF16)

    @pl.when(i < n_used_ref[0])
    def _():
        x = x_ref[...].astype(BF16)
        gu = jnp.dot(x, wgu_ref[0], preferred_element_type=F32) + bgu_ref[0]
        acts = []
        for c in range(gu.shape[1] // MXU_DIM):
            g = gu[:, c * MXU_DIM:c * MXU_DIM + LANES]
            u = gu[:, c * MXU_DIM + LANES:(c + 1) * MXU_DIM]
            glu = jnp.minimum(g, SWIGLU_LIMIT)
            lin = jnp.clip(u, -SWIGLU_LIMIT, SWIGLU_LIMIT)
            acts.append((glu * jax.nn.sigmoid(SWIGLU_ALPHA * glu) * (lin + 1.0)).astype(BF16))
        act = jnp.concatenate(acts, axis=1)
        o_ref[...] = jnp.dot(act, wd_scr[...], preferred_element_type=F32) + bd_ref[0, 0]

    @pl.when(i >= n_used_ref[0])
    def _():
        o_ref[...] = jnp.zeros_like(o_ref)


def moe_experts(xb, blk_e, n_used, wgu, bgu, w_down, b_down, layer):
    cap, D = xb.shape
    n_blocks = cap // MOE_BLOCK
    F2 = wgu.shape[-1]
    F = w_down.shape[2]
    grid_spec = pltpu.PrefetchScalarGridSpec(
        num_scalar_prefetch=2,
        grid=(n_blocks,),
        in_specs=[pl.BlockSpec((MOE_BLOCK, D), lambda i, e, n: (jnp.minimum(i, n[0] - 1), 0)),
                  pl.BlockSpec((1, D, F2), lambda i, e, n: (e[i], 0, 0)),
                  pl.BlockSpec((1, 1, F2), lambda i, e, n: (e[i], 0, 0)),
                  pl.BlockSpec((1, 1, F, D), lambda i, e, n: (layer, e[i], 0, 0)),
                  pl.BlockSpec((1, 1, 1, D), lambda i, e, n: (layer, e[i], 0, 0))],
        out_specs=pl.BlockSpec((MOE_BLOCK, D), lambda i, e, n: (i, 0)),
        scratch_shapes=[pltpu.VMEM((F, D), BF16)],
    )
    return pl.pallas_call(
        _moe_kernel,
        grid_spec=grid_spec,
        out_shape=jax.ShapeDtypeStruct((cap, D), F32),
        compiler_params=_params("arbitrary"),
        name="moe_experts",
    )(blk_e, n_used, xb, wgu, bgu, w_down, b_down)


def _combine_kernel(pc0, pc1, pc2, pc3, pn0, pn1, pn2, pn3, g_ref, x_ref, g2_ref, yb_hbm,
                    o_ref, buf, sem):
    i = pl.program_id(0)
    nb = pl.num_programs(0)
    tb = o_ref.shape[0]
    slot = i % 2

    def issue(pos_refs, s):
        for k in range(TOP_K):
            for t in range(tb):
                r = pos_refs[k][t]
                pltpu.make_async_copy(yb_hbm.at[pl.ds(r, 1), :],
                                      buf.at[s, pl.ds(k * tb + t, 1), :],
                                      sem.at[s]).start(priority=t % 2)

    @pl.when(i == 0)
    def _():
        issue((pc0, pc1, pc2, pc3), 0)

    @pl.when(i + 1 < nb)
    def _():
        issue((pn0, pn1, pn2, pn3), 1 - slot)

    pltpu.make_async_copy(yb_hbm.at[pl.ds(0, TOP_K * tb), :], buf.at[slot], sem.at[slot]).wait()
    y = jnp.zeros(o_ref.shape, F32)
    for k in range(TOP_K):
        y = y + g_ref[:, k:k + 1] * buf[slot, k * tb:(k + 1) * tb, :]
    o_ref[...] = x_ref[...] + g2_ref[0] * y


def moe_combine(yb, pos, gates, x1, mods, layer, n_lat):
    n, D = x1.shape
    tb = COMBINE_BLOCK
    nb = n // tb
    nlb = n_lat // tb
    sel = lambda i: jnp.where(i >= nlb, 1, 0)
    cur = pl.BlockSpec((tb,), lambda i: (i,), memory_space=pltpu.SMEM)
    nxt = pl.BlockSpec((tb,), lambda i: (jnp.minimum(i + 1, nb - 1),), memory_space=pltpu.SMEM)
    return pl.pallas_call(
        _combine_kernel,
        grid=(nb,),
        in_specs=[cur] * TOP_K + [nxt] * TOP_K + [
            pl.BlockSpec((tb, TOP_K), lambda i: (i, 0)),
            pl.BlockSpec((tb, D), lambda i: (i, 0)),
            _mod_spec(layer, sel, 5)(D),
            pl.BlockSpec(memory_space=pl.ANY)],
        out_specs=pl.BlockSpec((tb, D), lambda i: (i, 0)),
        out_shape=jax.ShapeDtypeStruct((n, D), F32),
        scratch_shapes=[pltpu.VMEM((2, TOP_K * tb, D), F32),
                        pltpu.SemaphoreType.DMA((2,))],
        compiler_params=_params("arbitrary"),
        name="moe_combine",
    )(pos[0], pos[1], pos[2], pos[3], pos[0], pos[1], pos[2], pos[3], gates, x1, mods, yb)


def _route_kernel(lg_ref, e_o, g_o, r_o, cnt_o, run_scr):
    i = pl.program_id(0)

    @pl.when(i == 0)
    def _():
        run_scr[...] = jnp.zeros_like(run_scr)

    tb = lg_ref.shape[0]
    lane = lax.broadcasted_iota(jnp.int32, (tb, LANES), 1)
    x = jnp.where(lane < N_EXPERTS, lg_ref[...], -jnp.inf)
    vals, hots = [], []
    for k in range(TOP_K):
        m = jnp.max(x, axis=-1, keepdims=True)
        idx = jnp.min(jnp.where(x == m, lane, LANES), axis=-1, keepdims=True)
        hot = lane == idx
        e_o[:, k:k + 1] = idx
        vals.append(m)
        hots.append(hot)
        x = jnp.where(hot, -jnp.inf, x)
    exps = [jnp.exp(v - vals[0]) for v in vals]
    denom = exps[0] + exps[1] + exps[2] + exps[3]
    for k in range(TOP_K):
        g_o[:, k:k + 1] = exps[k] / denom
    picked = (hots[0] | hots[1] | hots[2] | hots[3]).astype(BF16)
    r_idx = lax.broadcasted_iota(jnp.int32, (tb, tb), 0)
    c_idx = lax.broadcasted_iota(jnp.int32, (tb, tb), 1)
    before = (c_idx < r_idx).astype(BF16)
    seen = jnp.dot(before, picked, preferred_element_type=F32) + run_scr[...]
    for k in range(TOP_K):
        rank = jnp.sum(jnp.where(hots[k], seen, 0.0), axis=-1, keepdims=True)
        r_o[:, k:k + 1] = rank.astype(jnp.int32)
    run_scr[...] = run_scr[...] + jnp.sum(picked.astype(F32), axis=0, keepdims=True)
    cnt_o[...] = run_scr[...]


def route(logits):
    n = logits.shape[0]
    tb = ROW_BLOCK
    out = lambda: pl.BlockSpec((tb, TOP_K), lambda i: (i, 0))
    return pl.pallas_call(
        _route_kernel,
        grid=(n // tb,),
        in_specs=[pl.BlockSpec((tb, LANES), lambda i: (i, 0))],
        out_specs=[out(), out(), out(), pl.BlockSpec((1, LANES), lambda i: (0, 0))],
        out_shape=[jax.ShapeDtypeStruct((n, TOP_K), jnp.int32),
                   jax.ShapeDtypeStruct((n, TOP_K), F32),
                   jax.ShapeDtypeStruct((n, TOP_K), jnp.int32),
                   jax.ShapeDtypeStruct((1, LANES), F32)],
        scratch_shapes=[pltpu.VMEM((1, LANES), F32)],
        compiler_params=_params("arbitrary"),
        name="route",
    )(logits)


def _dispatch_kernel(pe_ref, cnt_ref, p0, p1, p2, p3, x_ref, xb_hbm, src, zeros, sem, zsem):
    i = pl.program_id(0)
    nb = pl.num_programs(0)
    tb = x_ref.shape[0]
    slot = i % 2
    pos_refs = (p0, p1, p2, p3)

    def zero_copy(row0):
        return pltpu.make_async_copy(zeros, xb_hbm.at[pl.ds(row0, MOE_BLOCK), :], zsem)

    @pl.when(i == 0)
    def _():
        zeros[...] = jnp.zeros_like(zeros)
        for e in range(N_EXPERTS):
            @pl.when(cnt_ref[e] > 0)
            def _(e=e):
                zero_copy(pl.multiple_of(pe_ref[e] - MOE_BLOCK, MOE_BLOCK)).start()
        n_used = pe_ref[N_EXPERTS - 1] // MOE_BLOCK
        n_blocks = xb_hbm.shape[0] // MOE_BLOCK

        def fill(b, carry):
            zero_copy(pl.multiple_of(b * MOE_BLOCK, MOE_BLOCK)).start()
            return carry

        def drain(b, carry):
            zero_copy(0).wait()
            return carry

        lax.fori_loop(n_used, n_blocks, fill, 0)
        lax.fori_loop(n_used, n_blocks, drain, 0)
        for e in range(N_EXPERTS):
            @pl.when(cnt_ref[e] > 0)
            def _():
                zero_copy(0).wait()

    def wait_slot(s):
        for _ in range(TOP_K):
            pltpu.make_async_copy(src.at[s], xb_hbm.at[pl.ds(0, tb), :], sem.at[s]).wait()

    @pl.when(i >= 2)
    def _():
        wait_slot(slot)

    src[slot] = x_ref[...]
    for k in range(TOP_K):
        for t in range(tb):
            r = pos_refs[k][t]
            pltpu.make_async_copy(src.at[slot, pl.ds(t, 1), :], xb_hbm.at[pl.ds(r, 1), :],
                                  sem.at[slot]).start(priority=(k + t) % 2)

    @pl.when(i == nb - 1)
    def _():
        wait_slot(slot)

    @pl.when((i == nb - 1) & (i >= 1))
    def _():
        wait_slot(1 - slot)


def moe_dispatch(h2, pos, pad_end, counts, cap):
    n, D = h2.shape
    tb = COMBINE_BLOCK
    grid_spec = pltpu.PrefetchScalarGridSpec(
        num_scalar_prefetch=2,
        grid=(n // tb,),
        in_specs=[pl.BlockSpec((tb,), lambda i, pe, c: (i,), memory_space=pltpu.SMEM)] * TOP_K + [
            pl.BlockSpec((tb, D), lambda i, pe, c: (i, 0))],
        out_specs=pl.BlockSpec(memory_space=pl.ANY),
        scratch_shapes=[pltpu.VMEM((2, tb, D), F32),
                        pltpu.VMEM((MOE_BLOCK, D), F32),
                        pltpu.SemaphoreType.DMA((2,)),
                        pltpu.SemaphoreType.DMA(())],
    )
    return pl.pallas_call(
        _dispatch_kernel,
        grid_spec=grid_spec,
        out_shape=jax.ShapeDtypeStruct((cap, D), F32),
        compiler_params=_params("arbitrary"),
        name="moe_dispatch",
    )(pad_end, counts, pos[0], pos[1], pos[2], pos[3], h2)


def moe_ffn(x1, h2, logits, wgu, bgu, w_down, b_down, mods, layer, n_lat):
    n, D = h2.shape
    top_e, gates, rank, cnt = route(logits)
    n_assign = n * TOP_K
    counts = cnt[0, :N_EXPERTS].astype(jnp.int32)
    padded = (counts + MOE_BLOCK - 1) // MOE_BLOCK * MOE_BLOCK
    pad_end = jnp.cumsum(padded)
    pad_start = pad_end - padded
    hot = top_e[:, :, None] == jnp.arange(N_EXPERTS, dtype=jnp.int32)[None, None, :]
    pos = (rank + jnp.sum(jnp.where(hot, pad_start[None, None, :], 0), axis=-1)).T
    n_blocks = -(-n_assign // MOE_BLOCK) + N_EXPERTS
    cap = n_blocks * MOE_BLOCK
    blk_row0 = jnp.arange(n_blocks, dtype=jnp.int32) * MOE_BLOCK
    blk_e = jnp.minimum(jnp.sum((pad_end[None, :] <= blk_row0[:, None]).astype(jnp.int32), axis=1),
                        N_EXPERTS - 1)
    n_used = pad_end[-1:] // MOE_BLOCK
    xb = moe_dispatch(h2, pos, pad_end, counts, cap)
    yb = moe_experts(xb, blk_e, n_used, wgu, bgu, w_down, b_down, layer)
    return moe_combine(yb, pos, gates, x1, mods, layer, n_lat)


def _rope_tables(n_lat, n_ctx):
    quarter = ROT_DIM // 4
    inv = 1.0 / (ROPE_THETA ** (jnp.arange(quarter, dtype=F32) / quarter))
    t = jnp.arange(n_lat)
    row = (t // GRID_W).astype(F32)
    col = (t % GRID_W).astype(F32)
    ang = jnp.concatenate([row[:, None] * inv, col[:, None] * inv], axis=-1)
    ang = jnp.concatenate([ang, jnp.zeros((n_ctx, ROT_DIM // 2), F32)], axis=0)
    cos, sin = jnp.cos(ang), jnp.sin(ang)
    reps = LANES // ROT_DIM
    cos_t = jnp.tile(jnp.concatenate([cos, cos], axis=-1), (1, reps))
    sin_t = jnp.tile(jnp.concatenate([-sin, sin], axis=-1), (1, reps))
    return cos_t, sin_t


def _pad_w_in(w):
    L, D, _ = w.shape
    a = MLA_Q_RANK + MLA_KV_RANK + MLA_ROPE
    return jnp.concatenate([w[..., :a], jnp.zeros((L, D, _P_DQ - a), w.dtype), w[..., a:]], axis=-1)


def _pad_w_uq(w):
    L, r, _ = w.shape
    w4 = w.reshape(L, r, MLA_HEADS, MLA_NOPE + MLA_ROPE)
    w4 = jnp.concatenate(
        [w4, jnp.zeros((L, r, MLA_HEADS, MLA_HEAD_PAD - MLA_NOPE - MLA_ROPE), w.dtype)], axis=-1)
    return w4.reshape(L, r, MLA_HEADS * MLA_HEAD_PAD)


def _row3(a, width=None, tile=1):
    a = jnp.tile(a.astype(F32), (1, tile))
    if width is not None and width > a.shape[1]:
        a = jnp.concatenate([a, jnp.zeros((a.shape[0], width - a.shape[1]), F32)], axis=1)
    return a[:, None, :]


def kernel(x, c, ctx, c_ctx, w_ada, b_ada, attn_norm, ffn_norm, w_in, mla_qa_norm, w_uq,
           mla_kva_norm, w_ukv, mla_q_gain, mla_knope_gain, mla_kpe_gain, diff_q_gain, diff_k_gain,
           diff_lambda, diff_subln, na_q_gain, na_k_gain, na_rpb, w_out, router_w, router_b,
           w_gu, b_gu, w_down, b_down):
    depth = w_in.shape[0]
    B, N, D = x.shape
    assert B == 1
    C = ctx.shape[1]
    T = N + C
    assert N % ROW_BLOCK == 0 and C % ROW_BLOCK == 0
    rows = N // GRID_W
    cos_t, sin_t = _rope_tables(N, C)
    silu = jnp.stack([jax.nn.silu(c[0]), jax.nn.silu(c_ctx)])
    silu_pad = jnp.concatenate([silu, jnp.zeros((14, D), F32)], axis=0).astype(BF16)
    n_groups = w_gu.shape[-1] // MXU_DIM

    mods = ada_all(silu_pad, w_ada)[:, :2] + b_ada[:, None, :]
    mods = mods.reshape(depth * 2 * N_MOD, 1, D)

    w_in_p = _pad_w_in(w_in).astype(BF16)
    w_uq_p = _pad_w_uq(w_uq).astype(BF16)
    w_ukv_b = w_ukv.astype(BF16)
    w_out_b = w_out.astype(BF16)
    gains = (_row3(mla_qa_norm), _row3(mla_kva_norm), _row3(mla_q_gain, MLA_HEAD_PAD),
             _row3(mla_knope_gain), _row3(mla_kpe_gain, LANES),
             _row3(diff_q_gain, tile=2), _row3(diff_k_gain, tile=2),
             _row3(na_q_gain), _row3(na_k_gain))
    attn_g = _row3(attn_norm)
    ffn_g = _row3(ffn_norm)
    subln = _row3(diff_subln)
    rw = jnp.concatenate([router_w.astype(F32),
                          jnp.zeros((depth, D, LANES - N_EXPERTS), F32)], axis=-1)
    rw_hi = rw.astype(BF16)
    rw = jnp.concatenate([rw_hi, (rw - rw_hi.astype(F32)).astype(BF16)], axis=-1)
    rb = _row3(router_b, LANES)
    bgu_all = b_gu.astype(F32).reshape(depth, N_EXPERTS, n_groups, LANES, 2)
    bgu_all = bgu_all.transpose(0, 1, 2, 4, 3).reshape(depth, N_EXPERTS, 1, n_groups * MXU_DIM)
    bd_all = b_down.astype(F32)[:, :, None, :]
    na_bias = _na_bias_tables(na_rpb, rows)

    xa = jnp.concatenate([x[0], ctx[0]], axis=0)
    for l in range(depth):
        with_ctx_out = l < depth - 1
        lam_init = 0.8 - 0.6 * math.exp(-0.3 * l)
        q_a, k_a, v_a, q_b, k_b, v_b, q_c, k_c, v_c = in_proj(
            xa, attn_g, mods, w_in_p, cos_t, sin_t, gains, w_uq_p, w_ukv_b, l, N)

        lf = diff_lambda[l].astype(F32)
        lam = jnp.exp(jnp.sum(lf[0] * lf[1])) - jnp.exp(jnp.sum(lf[2] * lf[3])) + lam_init
        diff_args = (jnp.full((1, LANES), lam, F32), subln, l, 1.0 - lam_init)

        o_a = flash(q_a, k_a, v_a, heads=MLA_HEADS, dk=MLA_HEAD_PAD,
                    n_q=N, q_start=0, n_k=T, k_start=0, tq_target=512, tk_target=1280, name="mla")
        o_b = flash(q_b, k_b, v_b, heads=DIFF_HEADS, dk=2 * DIFF_QK,
                    n_q=N, q_start=0, n_k=T, k_start=0, tq_target=256, tk_target=1280,
                    diff_args=diff_args, name="diff")
        o_c = neighborhood(q_c, k_c, v_c, na_bias, N, l)
        o_ctx = None
        wgu = deinterleave_gu(w_gu, l)
        if with_ctx_out:
            v_cc = _with_ones_column(v_c[N:].reshape(C, NA_HEADS, NA_DIM))
            oc_a = flash(q_a, k_a, v_a, heads=MLA_HEADS, dk=MLA_HEAD_PAD,
                         n_q=C, q_start=N, n_k=C, k_start=N, tq_target=256, tk_target=256, name="mla_ctx")
            oc_b = flash(q_b, k_b, v_b, heads=DIFF_HEADS, dk=2 * DIFF_QK,
                         n_q=C, q_start=N, n_k=C, k_start=N, tq_target=256, tk_target=256,
                         diff_args=diff_args, name="diff_ctx")
            oc_c = flash(q_c, k_c[N:], v_cc, heads=NA_HEADS, dk=NA_DIM,
                         n_q=C, q_start=N, n_k=C, k_start=0, tq_target=256, tk_target=256, name="na_ctx")
            o_ctx = (oc_a, oc_b, oc_c)
        x1, h2, logits = out_proj((o_a, o_b, o_c), o_ctx, xa, w_out_b, mods, ffn_g, rw, rb, l)
        xa = moe_ffn(x1, h2, logits, wgu, bgu_all[l], w_down, bd_all, mods, l, N)
    return xa[:N][None]
```
